```python
import jax, jax.numpy as jnp
from jax import lax
import numpy as np

D_MODEL = 2048
BATCH = 4
SEQ = 4096
DEPTH = 1

HEAD_DIM = 128
H_A = 8
KV_A = 2
G_A = H_A // KV_A
H_B = 8
KV_B = 2
G_B = H_B // KV_B
MIX_WIDTH = (H_A + H_B) * HEAD_DIM
WINDOW = 128
BLOCK = 128
GRID_W = 64
ROPE_THETA = 10000.0
N_EXPERTS = 16
CAPACITY_FACTOR = 2
D_FF = 2 * D_MODEL
EPS = 1e-6
NEG_INF = -1e30
SCALE = HEAD_DIM ** -0.5
COLS = [H_A * HEAD_DIM, KV_A * HEAD_DIM, KV_A * HEAD_DIM,
        H_B * HEAD_DIM, KV_B * HEAD_DIM, KV_B * HEAD_DIM]
D_IN = sum(COLS)
SPLITS = [int(v) for v in np.cumsum(COLS)[:-1]]

kernel_name = "hymba_window_sink_axialrope_ec_moe"


def rms_norm(x, g):
    xf = x.astype(jnp.float32)
    y = xf * lax.rsqrt(jnp.mean(xf * xf, axis=-1, keepdims=True) + EPS)
    return (y * g.astype(jnp.float32)).astype(x.dtype)


def alibi_slopes(n):
    return jnp.asarray(2.0 ** (-8.0 * np.arange(1, n + 1) / n), dtype=jnp.float32)


def windowed_sink_attention(q, k, v, sink):
    B, S = q.shape[0], q.shape[1]
    nb = S // BLOCK
    qb = q.reshape(B, nb, BLOCK, KV_A, G_A, HEAD_DIM)

    def band(t):
        tp = jnp.pad(t, ((0, 0), (BLOCK, BLOCK), (0, 0), (0, 0)))
        tp = tp.reshape(B, nb + 2, BLOCK, KV_A, HEAD_DIM)
        return jnp.concatenate([tp[:, :-2], tp[:, 1:-1], tp[:, 2:]], axis=2)

    kw, vw = band(k), band(v)
    s = jnp.einsum('bnqkgd,bnjkd->bnkgqj', qb, kw,
                   preferred_element_type=jnp.float32) * SCALE
    qi = jnp.arange(BLOCK)
    kj = jnp.arange(3 * BLOCK)
    dist = qi[:, None] + BLOCK - kj[None, :]
    kpos = jnp.arange(nb)[:, None] * BLOCK - BLOCK + kj[None, :]
    valid = (jnp.abs(dist) <= WINDOW)[None] & ((kpos >= 0) & (kpos < S))[:, None, :]
    slopes = alibi_slopes(H_A).reshape(KV_A, G_A)
    bias = -slopes[:, :, None, None] * jnp.abs(dist).astype(jnp.float32)[None, None]
    s = jnp.where(valid[None, :, None, None], s + bias[None, None], NEG_INF)
    sinkc = jnp.broadcast_to(sink.astype(jnp.float32).reshape(KV_A, G_A, 1, 1),
                             s.shape[:-1] + (1,))
    p = jax.nn.softmax(jnp.concatenate([s, sinkc], axis=-1), axis=-1)[..., :-1]
    o = jnp.einsum('bnkgqj,bnjkd->bnqkgd', p.astype(v.dtype), vw)
    return o.reshape(B, S, H_A * HEAD_DIM)


def axial_rope(t, row, col):
    half = HEAD_DIM // 2
    inv_freq = ROPE_THETA ** (-jnp.arange(0, half, 2, dtype=jnp.float32) / half)

    def rotate(seg, pos):
        ang = pos.astype(jnp.float32)[:, None] * inv_freq[None, :]
        cos = jnp.cos(ang)[None, :, None, :]
        sin = jnp.sin(ang)[None, :, None, :]
        s1, s2 = jnp.split(seg.astype(jnp.float32), 2, axis=-1)
        return jnp.concatenate([s1 * cos - s2 * sin, s2 * cos + s1 * sin], axis=-1)

    out = jnp.concatenate([rotate(t[..., :half], row), rotate(t[..., half:], col)], axis=-1)
    return out.astype(t.dtype)


def grid_attention(q, k, v):
    B, S = q.shape[0], q.shape[1]
    nb = S // BLOCK
    qb = jnp.moveaxis(q.reshape(B, nb, BLOCK, KV_B, G_B, HEAD_DIM), 1, 0)

    def one_block(qblk):
        s = jnp.einsum('bqkgd,bskd->bkgqs', qblk, k,
                       preferred_element_type=jnp.float32) * SCALE
        p = jax.nn.softmax(s, axis=-1)
        return jnp.einsum('bkgqs,bskd->bqkgd', p.astype(v.dtype), v)

    o = lax.map(one_block, qb)
    return jnp.moveaxis(o, 0, 1).reshape(B, S, H_B * HEAD_DIM)


def expert_choice_ffn(h, w_router, w_gate, w_up, w_down):
    B, S, _ = h.shape
    cap = CAPACITY_FACTOR * S // N_EXPERTS
    logits = jnp.einsum('bsd,de->bse', h, w_router, preferred_element_type=jnp.float32)
    aff = jax.nn.softmax(logits, axis=-1)
    gates, idx = lax.top_k(jnp.swapaxes(aff, 1, 2), cap)
    b_idx = jnp.arange(B)[:, None, None]
    xg = h[b_idx, idx]
    a = jnp.einsum('becd,edf->becf', xg, w_gate)
    u = jnp.einsum('becd,edf->becf', xg, w_up)
    eo = jnp.einsum('becf,efd->becd', jax.nn.silu(a) * u, w_down)
    eo = eo * gates[..., None].astype(eo.dtype)
    return jnp.zeros_like(h).at[b_idx, idx].add(eo)


def hybrid_layer(x, row, col, norm_mix, w_in, sink_a, q_norm_b, k_norm_b, w_out,
                 norm_ffn, w_router, w_gate, w_up, w_down):
    B, S, _ = x.shape
    h = rms_norm(x, norm_mix)
    proj = jnp.einsum('bsd,de->bse', h, w_in)
    qa, ka, va, qb, kb, vb = jnp.split(proj, SPLITS, axis=-1)
    qa = qa.reshape(B, S, H_A, HEAD_DIM)
    ka = ka.reshape(B, S, KV_A, HEAD_DIM)
    va = va.reshape(B, S, KV_A, HEAD_DIM)
    qb = axial_rope(rms_norm(qb.reshape(B, S, H_B, HEAD_DIM), q_norm_b), row, col)
    kb = axial_rope(rms_norm(kb.reshape(B, S, KV_B, HEAD_DIM), k_norm_b), row, col)
    vb = vb.reshape(B, S, KV_B, HEAD_DIM)
    oa = windowed_sink_attention(qa, ka, va, sink_a)
    ob = grid_attention(qb, kb, vb)
    mix = jnp.einsum('bse,ed->bsd', jnp.concatenate([oa, ob], axis=-1), w_out)
    x = x + mix
    x = x + expert_choice_ffn(rms_norm(x, norm_ffn), w_router, w_gate, w_up, w_down)
    return x


def setup_inputs(seed: int = 0) -> dict:
    key = jax.random.key(seed)
    ks = jax.random.split(key, 13)
    f32 = jnp.float32
    L = DEPTH
    return {
        'x': jax.random.normal(ks[0], (BATCH, SEQ, D_MODEL), f32),
        'norm_mix': 1.0 + 0.02 * jax.random.normal(ks[1], (L, D_MODEL), f32),
        'w_in': jax.random.normal(ks[2], (L, D_MODEL, D_IN), f32) * D_MODEL ** -0.5,
        'sink_a': 0.5 * jax.random.normal(ks[3], (L, H_A), f32),
        'q_norm_b': 1.0 + 0.02 * jax.random.normal(ks[4], (L, HEAD_DIM), f32),
        'k_norm_b': 1.0 + 0.02 * jax.random.normal(ks[5], (L, HEAD_DIM), f32),
        'w_out': jax.random.normal(ks[6], (L, MIX_WIDTH, D_MODEL), f32) * MIX_WIDTH ** -0.5,
        'norm_ffn': 1.0 + 0.02 * jax.random.normal(ks[7], (L, D_MODEL), f32),
        'w_router': jax.random.normal(ks[8], (L, D_MODEL, N_EXPERTS), f32) * D_MODEL ** -0.5,
        'w_gate': jax.random.normal(ks[9], (L, N_EXPERTS, D_MODEL, D_FF), f32) * D_MODEL ** -0.5,
        'w_up': jax.random.normal(ks[10], (L, N_EXPERTS, D_MODEL, D_FF), f32) * D_MODEL ** -0.5,
        'w_down': jax.random.normal(ks[11], (L, N_EXPERTS, D_FF, D_MODEL), f32) * D_FF ** -0.5,
        'norm_final': 1.0 + 0.02 * jax.random.normal(ks[12], (D_MODEL,), f32),
    }


def reference(x, norm_mix, w_in, sink_a, q_norm_b, k_norm_b, w_out, norm_ffn,
              w_router, w_gate, w_up, w_down, norm_final):
    S = x.shape[1]
    rows = S // GRID_W
    row = jnp.broadcast_to(jnp.arange(rows)[:, None], (rows, GRID_W)).reshape(S)
    col = jnp.broadcast_to(jnp.arange(GRID_W)[None, :], (rows, GRID_W)).reshape(S)
    for l in range(DEPTH):
        x = hybrid_layer(x, row, col, norm_mix[l], w_in[l], sink_a[l], q_norm_b[l],
                         k_norm_b[l], w_out[l], norm_ffn[l], w_router[l], w_gate[l],
                         w_up[l], w_down[l])
    return rms_norm(x, norm_final)
```

```python
import functools
from typing import NamedTuple

import numpy as np
import jax
import jax.numpy as jnp
from jax import lax
from jax.experimental import pallas as pl
from jax.experimental.pallas import tpu as pltpu

HEAD_DIM = 128
H_A, KV_A = 8, 2
H_B, KV_B = 8, 2
GROUP = 4
WINDOW = 128
BLOCK = 128
GRID_W = 64
ROPE_THETA = 10000.0
EPS = 1e-6
NEG_INF = -1e30
SCALE = HEAD_DIM ** -0.5
LANES = 128
SUBLANES = 8
N_PROJ_HEADS = H_A + 2 * KV_A + H_B + 2 * KV_B
MIB = 1024 * 1024

F32 = jnp.float32
BF16 = jnp.bfloat16
I32 = jnp.int32


class Cfg(NamedTuple):
    batch: int
    seq: int
    d_model: int
    d_ff: int
    n_experts: int
    cap: int


def _params(sem, vmem_mib):
    return pltpu.CompilerParams(dimension_semantics=sem, vmem_limit_bytes=vmem_mib * MIB)


def _resident(shape, index_map):
    return pl.BlockSpec(shape, index_map, pipeline_mode=pl.Buffered(1))


def _inproj_body(x_ref, g_ref, w_ref, qn_ref, kn_ref, cos_ref, sin_ref,
                 qa_ref, ka_ref, va_ref, qb_ref, kb_ref, vb_ref):
    xf = x_ref[...]
    ms = jnp.mean(xf * xf, axis=-1, keepdims=True)
    h = (xf * lax.rsqrt(ms + EPS) * g_ref[...]).astype(BF16)
    proj = jnp.dot(h, w_ref[...], preferred_element_type=F32)
    cos = cos_ref[...]
    sin = sin_ref[...]
    lane = lax.broadcasted_iota(I32, (1, LANES), 1)
    first = (lane % 64) < 32

    def head(c):
        return proj[:, c * HEAD_DIM:(c + 1) * HEAD_DIM]

    def norm_rope(t, gain):
        m = jnp.mean(t * t, axis=-1, keepdims=True)
        tn = t * lax.rsqrt(m + EPS) * gain
        partner = jnp.where(first, pltpu.roll(tn, 96, 1), pltpu.roll(tn, 32, 1))
        return tn * cos + partner * sin

    c = 0
    for hh in range(H_A):
        qa_ref[hh] = (head(c + hh) * SCALE).astype(BF16)
    c += H_A
    for hh in range(KV_A):
        ka_ref[hh] = head(c + hh).astype(BF16)
    c += KV_A
    for hh in range(KV_A):
        va_ref[hh] = head(c + hh).astype(BF16)
    c += KV_A
    for hh in range(H_B):
        qb_ref[hh] = (norm_rope(head(c + hh), qn_ref[...]) * SCALE).astype(BF16)
    c += H_B
    for hh in range(KV_B):
        kb_ref[hh] = norm_rope(head(c + hh), kn_ref[...]).astype(BF16)
    c += KV_B
    for hh in range(KV_B):
        vb_ref[hh] = head(c + hh).astype(BF16)


def _inproj(cfg, x2, norm_mix, w_in_bf, q_norm, k_norm, cos_t, sin_t):
    T, D = x2.shape
    tm = 256
    sb = cfg.seq // tm
    heads = lambda n: jax.ShapeDtypeStruct((n, T, HEAD_DIM), BF16)
    hspec = lambda n: pl.BlockSpec((n, tm, HEAD_DIM), lambda i: (0, i, 0))
    return pl.pallas_call(
        _inproj_body,
        grid=(T // tm,),
        in_specs=[
            pl.BlockSpec((tm, D), lambda i: (i, 0)),
            _resident((1, D), lambda i: (0, 0)),
            _resident((D, N_PROJ_HEADS * HEAD_DIM), lambda i: (0, 0)),
            _resident((1, HEAD_DIM), lambda i: (0, 0)),
            _resident((1, HEAD_DIM), lambda i: (0, 0)),
            pl.BlockSpec((tm, HEAD_DIM), lambda i: (i % sb, 0)),
            pl.BlockSpec((tm, HEAD_DIM), lambda i: (i % sb, 0)),
        ],
        out_specs=[hspec(H_A), hspec(KV_A), hspec(KV_A), hspec(H_B), hspec(KV_B), hspec(KV_B)],
        out_shape=[heads(H_A), heads(KV_A), heads(KV_A), heads(H_B), heads(KV_B), heads(KV_B)],
        compiler_params=_params(("parallel",), 40),
        name="inproj",
    )(x2, norm_mix, w_in_bf, q_norm, k_norm, cos_t, sin_t)


def _win_body(sink_ref, slope_ref, q_ref, k_ref, v_ref, o_ref, *, seq):
    kvh = pl.program_id(1)
    span = 3 * BLOCK

    def block(n, carry):
        ws = pl.multiple_of(jnp.clip((n - 1) * BLOCK, 0, seq - span), BLOCK)
        q0 = pl.multiple_of(n * BLOCK, BLOCK)
        kw = k_ref[0, pl.ds(ws, span), :]
        vw = v_ref[0, pl.ds(ws, span), :]
        qpos = q0 + lax.broadcasted_iota(I32, (BLOCK, span), 0)
        kpos = ws + lax.broadcasted_iota(I32, (BLOCK, span), 1)
        dist = jnp.abs(qpos - kpos)
        valid = dist <= WINDOW
        distf = dist.astype(F32)
        for g in range(GROUP):
            hq = kvh * GROUP + g
            q = q_ref[g, pl.ds(q0, BLOCK), :]
            s = lax.dot_general(q, kw, (((1,), (1,)), ((), ())), preferred_element_type=F32)
            s = jnp.where(valid, s - slope_ref[hq] * distf, NEG_INF)
            sink = sink_ref[hq]
            m = jnp.maximum(jnp.max(s, axis=-1, keepdims=True), sink)
            e = jnp.exp(s - m)
            den = jnp.sum(e, axis=-1, keepdims=True) + jnp.exp(sink - m)
            p = (e / den).astype(BF16)
            o = jnp.dot(p, vw, preferred_element_type=F32)
            o_ref[pl.ds(q0, BLOCK), g * HEAD_DIM:(g + 1) * HEAD_DIM] = o.astype(BF16)
        return carry

    lax.fori_loop(0, seq // BLOCK, block, 0)


def _win_attention(cfg, sink, slopes, qa, ka, va):
    S = cfg.seq
    T = cfg.batch * S
    smem = pl.BlockSpec(memory_space=pltpu.SMEM)
    return pl.pallas_call(
        functools.partial(_win_body, seq=S),
        grid=(cfg.batch, KV_A),
        in_specs=[
            smem, smem,
            pl.BlockSpec((GROUP, S, HEAD_DIM), lambda b, k: (k, b, 0)),
            pl.BlockSpec((1, S, HEAD_DIM), lambda b, k: (k, b, 0)),
            pl.BlockSpec((1, S, HEAD_DIM), lambda b, k: (k, b, 0)),
        ],
        out_specs=pl.BlockSpec((S, GROUP * HEAD_DIM), lambda b, k: (b, k)),
        out_shape=jax.ShapeDtypeStruct((T, H_A * HEAD_DIM), BF16),
        compiler_params=_params(("parallel", "parallel"), 40),
        name="win_attention",
    )(sink, slopes, qa, ka, va)


def _grid_body(q_ref, k_ref, v_ref, o_ref, m_scr, l_scr, acc_scr, *, seq, tq, tk):
    q = q_ref[...].reshape(GROUP * tq, HEAD_DIM)
    m_scr[...] = jnp.full(m_scr.shape, NEG_INF, F32)
    l_scr[...] = jnp.zeros(l_scr.shape, F32)
    acc_scr[...] = jnp.zeros(acc_scr.shape, F32)

    def chunk(j, carry):
        k0 = pl.multiple_of(j * tk, tk)
        k = k_ref[0, pl.ds(k0, tk), :]
        v = v_ref[0, pl.ds(k0, tk), :]
        s = lax.dot_general(q, k, (((1,), (1,)), ((), ())), preferred_element_type=F32)
        m_prev = m_scr[...]
        m_new = jnp.maximum(m_prev, jnp.max(s, axis=-1, keepdims=True))
        alpha = jnp.exp(m_prev - m_new)
        p = jnp.exp(s - m_new)
        l_scr[...] = alpha * l_scr[...] + jnp.sum(p, axis=-1, keepdims=True)
        acc_scr[...] = alpha * acc_scr[...] + jnp.dot(p.astype(BF16), v, preferred_element_type=F32)
        m_scr[...] = m_new
        return carry

    lax.fori_loop(0, seq // tk, chunk, 0)
    o = acc_scr[...] / l_scr[...]
    for g in range(GROUP):
        o_ref[:, g * HEAD_DIM:(g + 1) * HEAD_DIM] = o[g * tq:(g + 1) * tq].astype(BF16)


def _grid_attention(cfg, qb, kb, vb):
    S = cfg.seq
    T = cfg.batch * S
    tq, tk = 256, 512
    nq = S // tq
    rows = GROUP * tq
    return pl.pallas_call(
        functools.partial(_grid_body, seq=S, tq=tq, tk=tk),
        grid=(cfg.batch, KV_B, nq),
        in_specs=[
            pl.BlockSpec((GROUP, tq, HEAD_DIM), lambda b, k, i: (k, b * nq + i, 0)),
            pl.BlockSpec((1, S, HEAD_DIM), lambda b, k, i: (k, b, 0)),
            pl.BlockSpec((1, S, HEAD_DIM), lambda b, k, i: (k, b, 0)),
        ],
        out_specs=pl.BlockSpec((tq, GROUP * HEAD_DIM), lambda b, k, i: (b * nq + i, k)),
        out_shape=jax.ShapeDtypeStruct((T, H_B * HEAD_DIM), BF16),
        scratch_shapes=[pltpu.VMEM((rows, 1), F32), pltpu.VMEM((rows, 1), F32),
                        pltpu.VMEM((rows, HEAD_DIM), F32)],
        compiler_params=_params(("parallel", "parallel", "arbitrary"), 40),
        name="grid_attention",
    )(qb, kb, vb)


def _outproj_body(oa_ref, ob_ref, x_ref, wa_ref, wb_ref, g_ref, wr_ref, wrt_ref,
                  x1_ref, h_ref, aff_ref, afft_ref, *, tm, half_tiles):
    mix = jnp.dot(oa_ref[...], wa_ref[...], preferred_element_type=F32)
    mix = mix + jnp.dot(ob_ref[...], wb_ref[...], preferred_element_type=F32)
    x1 = x_ref[...] + mix
    x1_ref[...] = x1
    ms = jnp.mean(x1 * x1, axis=-1, keepdims=True)
    h = x1 * lax.rsqrt(ms + EPS) * g_ref[...]
    logits = jnp.dot(h, wr_ref[...], precision=lax.Precision.HIGHEST, preferred_element_type=F32)
    e = jnp.exp(logits - jnp.max(logits, axis=-1, keepdims=True))
    aff_ref[...] = e / jnp.sum(e, axis=-1, keepdims=True)
    logits_t = lax.dot_general(wrt_ref[...], h, (((1,), (1,)), ((), ())),
                               precision=lax.Precision.HIGHEST, preferred_element_type=F32)
    et = jnp.exp(logits_t - jnp.max(logits_t, axis=0, keepdims=True))
    afft_ref[...] = et / jnp.sum(et, axis=0, keepdims=True)
    for half in range(2):
        for j in range(half_tiles):
            c0 = (half * half_tiles + j) * LANES
            h_ref[0, half, pl.ds(j, tm, stride=half_tiles), :] = h[:, c0:c0 + LANES]


def _outproj(cfg, oa, ob, x2, w_a, w_b, norm_ffn, w_router, w_router_t):
    T, D = x2.shape
    E = cfg.n_experts
    S = cfg.seq
    tm = 256
    sb = S // tm
    ht = D // (2 * LANES)
    return pl.pallas_call(
        functools.partial(_outproj_body, tm=tm, half_tiles=ht),
        grid=(T // tm,),
        in_specs=[
            pl.BlockSpec((tm, oa.shape[1]), lambda i: (i, 0)),
            pl.BlockSpec((tm, ob.shape[1]), lambda i: (i, 0)),
            pl.BlockSpec((tm, D), lambda i: (i, 0)),
            _resident(w_a.shape, lambda i: (0, 0)),
            _resident(w_b.shape, lambda i: (0, 0)),
            _resident((1, D), lambda i: (0, 0)),
            _resident((D, E), lambda i: (0, 0)),
            _resident((E, D), lambda i: (0, 0)),
        ],
        out_specs=[
            pl.BlockSpec((tm, D), lambda i: (i, 0)),
            pl.BlockSpec((1, 2, tm * ht, LANES), lambda i: (i // sb, 0, i % sb, 0)),
            pl.BlockSpec((tm, E), lambda i: (i, 0)),
            pl.BlockSpec((E, tm), lambda i: (0, i)),
        ],
        out_shape=[
            jax.ShapeDtypeStruct((T, D), F32),
            jax.ShapeDtypeStruct((cfg.batch, 2, S * ht, LANES), F32),
            jax.ShapeDtypeStruct((T, E), F32),
            jax.ShapeDtypeStruct((E, T), F32),
        ],
        compiler_params=_params(("parallel",), 40),
        name="outproj_router",
    )(oa, ob, x2, w_a, w_b, norm_ffn, w_router, w_router_t)


def _topk_body(aff_ref, idx_ref, c_scr, *, seq, cap, n_exp):
    a = aff_ref[...]
    bits = lax.bitcast_convert_type(a, I32)

    def bisect(i, lo):
        cand = lo | jnp.left_shift(jnp.int32(1), 30 - i)
        cnt = jnp.sum((bits >= cand).astype(I32), axis=0, keepdims=True)
        return jnp.where(cnt >= cap, cand, lo)

    thr = lax.fori_loop(0, 31, bisect, jnp.zeros((1, n_exp), I32))
    gt = bits > thr
    eq = bits == thr
    need = (cap - jnp.sum(gt.astype(I32), axis=0, keepdims=True)).astype(F32)

    rc = 256
    r_i = lax.broadcasted_iota(I32, (rc, rc), 0)
    c_i = lax.broadcasted_iota(I32, (rc, rc), 1)
    lower = (c_i <= r_i).astype(BF16)

    def prefix(mask_f32):
        carry = jnp.zeros((1, n_exp), F32)
        for k in range(seq // rc):
            blk = mask_f32[k * rc:(k + 1) * rc, :]
            inc = jnp.dot(lower, blk.astype(BF16), preferred_element_type=F32) + carry
            c_scr[k * rc:(k + 1) * rc, :] = inc
            carry = inc[rc - 1:rc, :]
        return c_scr[...]

    eq_f = eq.astype(F32)
    tie_rank = prefix(eq_f) - eq_f
    sel = jnp.logical_or(gt, jnp.logical_and(eq, tie_rank < need))
    prefix(sel.astype(F32))

    r_row = lax.broadcasted_iota(I32, (1, cap), 1).astype(F32)
    rows = 512
    for e in range(n_exp):
        def count(k, acc, e=e):
            r0 = pl.multiple_of(k * rows, rows)
            col = c_scr[pl.ds(r0, rows), e:e + 1]
            hit = (col <= r_row).astype(F32)
            return acc + jnp.sum(hit.reshape(rows // SUBLANES, SUBLANES, cap), axis=0)

        acc = lax.fori_loop(0, seq // rows, count, jnp.zeros((SUBLANES, cap), F32))
        idx_ref[0, e:e + 1, :] = jnp.sum(acc, axis=0, keepdims=True).astype(I32)


def _topk(cfg, aff):
    S, E, C = cfg.seq, cfg.n_experts, cfg.cap
    return pl.pallas_call(
        functools.partial(_topk_body, seq=S, cap=C, n_exp=E),
        grid=(cfg.batch,),
        in_specs=[pl.BlockSpec((S, E), lambda b: (b, 0))],
        out_specs=pl.BlockSpec((1, E, C), lambda b: (b, 0, 0)),
        out_shape=jax.ShapeDtypeStruct((cfg.batch, E, C), I32),
        scratch_shapes=[pltpu.VMEM((S, E), F32)],
        compiler_params=_params(("parallel",), 40),
        name="topk",
    )(aff)


def _gather_body(idx_ref, h_ref, o_ref, tile_scr, *, cap, half_tiles, stride):
    for m in range(cap):
        t = idx_ref[0, 0, m]
        row = pl.multiple_of(t * half_tiles, half_tiles)
        tile_scr[pl.ds(m, half_tiles, stride=stride), :] = h_ref[0, 0, pl.ds(row, half_tiles), :]
    for j in range(half_tiles):
        o_ref[0, :, j * LANES:(j + 1) * LANES] = tile_scr[pl.ds(j * stride, cap), :].astype(BF16)


def _gather(cfg, idx3, h_tm):
    B, S, E, C, D = cfg.batch, cfg.seq, cfg.n_experts, cfg.cap, cfg.d_model
    ht = D // (2 * LANES)
    stride = C + SUBLANES
    return pl.pallas_call(
        functools.partial(_gather_body, cap=C, half_tiles=ht, stride=stride),
        grid=(B, 2, E),
        in_specs=[
            pl.BlockSpec((1, 1, C), lambda b, hf, e: (b * E + e, 0, 0), memory_space=pltpu.SMEM),
            _resident((1, 1, S * ht, LANES), lambda b, hf, e: (b, hf, 0, 0)),
        ],
        out_specs=pl.BlockSpec((1, C, D // 2), lambda b, hf, e: (e, b, hf)),
        out_shape=jax.ShapeDtypeStruct((E, B * C, D), BF16),
        scratch_shapes=[pltpu.VMEM((ht * stride, LANES), F32)],
        compiler_params=_params(("parallel", "parallel", "arbitrary"), 40),
        name="gather_tokens",
    )(idx3, h_tm)


def _ffn_body(x_ref, wg_ref, wu_ref, wd_ref, o_ref, acc_scr, *, tm, tiles, n_f):
    f = pl.program_id(2)
    x = x_ref[0]
    a = jnp.dot(x, wg_ref[0].astype(BF16), preferred_element_type=F32)
    u = jnp.dot(x, wu_ref[0].astype(BF16), preferred_element_type=F32)
    hid = (a * jax.nn.sigmoid(a) * u).astype(BF16)
    y = jnp.dot(hid, wd_ref[0].astype(BF16), preferred_element_type=F32)

    @pl.when(f == 0)
    def _():
        acc_scr[...] = y

    @pl.when(f > 0)
    def _():
        acc_scr[...] += y

    @pl.when(f == n_f - 1)
    def _():
        for j in range(tiles):
            o_ref[0, pl.ds(j, tm, stride=tiles), :] = acc_scr[:, j * LANES:(j + 1) * LANES]


def _ffn(cfg, xg, w_gate, w_up, w_down):
    E, D, F = cfg.n_experts, cfg.d_model, cfg.d_ff
    M = cfg.batch * cfg.cap
    tm = min(1024, M)
    tf = min(256, F)
    tiles = D // LANES
    n_f = F // tf
    return pl.pallas_call(
        functools.partial(_ffn_body, tm=tm, tiles=tiles, n_f=n_f),
        grid=(E, M // tm, n_f),
        in_specs=[
            pl.BlockSpec((1, tm, D), lambda e, m, f: (e, m, 0)),
            pl.BlockSpec((1, D, tf), lambda e, m, f: (e, 0, f)),
            pl.BlockSpec((1, D, tf), lambda e, m, f: (e, 0, f)),
            pl.BlockSpec((1, tf, D), lambda e, m, f: (e, f, 0)),
        ],
        out_specs=pl.BlockSpec((1, tm * tiles, LANES), lambda e, m, f: (e, m, 0)),
        out_shape=jax.ShapeDtypeStruct((E, M * tiles, LANES), F32),
        scratch_shapes=[pltpu.VMEM((tm, D), F32)],
        compiler_params=_params(("parallel", "parallel", "arbitrary"), 52),
        name="expert_ffn",
    )(xg, w_gate, w_up, w_down)


def _combine_body(idx_ref, gate_ref, eo_ref, o_ref, *, cap, tiles, th, steps):
    tok0 = pl.program_id(1) * th
    e = pl.program_id(2)

    @pl.when(e == 0)
    def _():
        o_ref[...] = jnp.zeros(o_ref.shape, F32)

    def lower_bound(v):
        def step(_, lh):
            lo, hi = lh
            mid = jnp.minimum((lo + hi) // 2, cap - 1)
            below = idx_ref[0, 0, mid] < v
            return jnp.where(below, mid + 1, lo), jnp.where(below, hi, mid)

        return lax.fori_loop(0, steps, step, (jnp.int32(0), jnp.int32(cap)))[0]

    first = lower_bound(tok0)
    last = lower_bound(tok0 + th)

    def add_row(m, carry):
        t = idx_ref[0, 0, m]
        gate = gate_ref[0, 0, t]
        dst = pl.multiple_of((t - tok0) * tiles, tiles)
        src = pl.multiple_of(m * tiles, tiles)
        o_ref[0, pl.ds(dst, tiles), :] = (o_ref[0, pl.ds(dst, tiles), :]
                                          + eo_ref[0, pl.ds(src, tiles), :] * gate)
        return carry

    lax.fori_loop(first, last, add_row, 0)


def _combine(cfg, idx3, gates3, eo_tm):
    B, S, E, C, D = cfg.batch, cfg.seq, cfg.n_experts, cfg.cap, cfg.d_model
    tiles = D // LANES
    th = min(2048, S)
    steps = int(np.ceil(np.log2(C))) + 1
    return pl.pallas_call(
        functools.partial(_combine_body, cap=C, tiles=tiles, th=th, steps=steps),
        grid=(B, S // th, E),
        in_specs=[
            pl.BlockSpec((1, 1, C), lambda b, h, e: (b * E + e, 0, 0), memory_space=pltpu.SMEM),
            pl.BlockSpec((1, 1, S), lambda b, h, e: (e * B + b, 0, 0), memory_space=pltpu.SMEM),
            pl.BlockSpec((1, C * tiles, LANES), lambda b, h, e: (e, b, 0)),
        ],
        out_specs=pl.BlockSpec((1, th * tiles, LANES), lambda b, h, e: (b, h, 0)),
        out_shape=jax.ShapeDtypeStruct((B, S * tiles, LANES), F32),
        compiler_params=_params(("parallel", "parallel", "arbitrary"), 48),
        name="combine",
    )(idx3, gates3, eo_tm)


def _final_body(f_ref, x_ref, g_ref, o_ref, *, tm, tiles, d_model):
    ss = jnp.zeros((tm, 1), F32)
    for j in range(tiles):
        cols = slice(j * LANES, (j + 1) * LANES)
        y = x_ref[:, cols] + f_ref[0, pl.ds(j, tm, stride=tiles), :]
        o_ref[:, cols] = y
        ss = ss + jnp.sum(y * y, axis=-1, keepdims=True)
    o_ref[...] = o_ref[...] * lax.rsqrt(ss / d_model + EPS) * g_ref[...]


def _final(cfg, ffn_tm, x1, norm_final):
    T, D = x1.shape
    tiles = D // LANES
    tm = 256
    sb = cfg.seq // tm
    return pl.pallas_call(
        functools.partial(_final_body, tm=tm, tiles=tiles, d_model=D),
        grid=(T // tm,),
        in_specs=[
            pl.BlockSpec((1, tm * tiles, LANES), lambda i: (i // sb, i % sb, 0)),
            pl.BlockSpec((tm, D), lambda i: (i, 0)),
            _resident((1, D), lambda i: (0, 0)),
        ],
        out_specs=pl.BlockSpec((tm, D), lambda i: (i, 0)),
        out_shape=jax.ShapeDtypeStruct((T, D), F32),
        compiler_params=_params(("parallel",), 40),
        name="final_norm",
    )(ffn_tm, x1, norm_final)


def _rope_tables(seq):
    rows = seq // GRID_W
    row = jnp.broadcast_to(jnp.arange(rows)[:, None], (rows, GRID_W)).reshape(seq)
    col = jnp.broadcast_to(jnp.arange(GRID_W)[None, :], (rows, GRID_W)).reshape(seq)
    half = HEAD_DIM // 2
    inv_freq = ROPE_THETA ** (-jnp.arange(0, half, 2, dtype=F32) / half)
    ang_r = row.astype(F32)[:, None] * inv_freq[None, :]
    ang_c = col.astype(F32)[:, None] * inv_freq[None, :]
    cos_t = jnp.concatenate([jnp.cos(ang_r), jnp.cos(ang_r), jnp.cos(ang_c), jnp.cos(ang_c)], axis=-1)
    sin_t = jnp.concatenate([-jnp.sin(ang_r), jnp.sin(ang_r), -jnp.sin(ang_c), jnp.sin(ang_c)], axis=-1)
    return cos_t, sin_t


def _layer(cfg, x2, cos_t, sin_t, slopes, norm_mix, w_in, sink_a, q_norm_b, k_norm_b, w_out,
           norm_ffn, w_router, w_gate, w_up, w_down):
    B, S, E, C = cfg.batch, cfg.seq, cfg.n_experts, cfg.cap
    qa, ka, va, qb, kb, vb = _inproj(cfg, x2, norm_mix[None], w_in.astype(BF16),
                                     q_norm_b[None], k_norm_b[None], cos_t, sin_t)
    oa = _win_attention(cfg, sink_a, slopes, qa, ka, va)
    ob = _grid_attention(cfg, qb, kb, vb)
    w_out_bf = w_out.astype(BF16)
    n_a = H_A * HEAD_DIM
    x1, h_tm, aff, aff_t = _outproj(cfg, oa, ob, x2, w_out_bf[:n_a], w_out_bf[n_a:],
                                    norm_ffn[None], w_router, w_router.T)
    idx = _topk(cfg, aff)
    idx3 = idx.reshape(B * E, 1, C)
    gates3 = aff_t.reshape(E * B, 1, S)
    xg = _gather(cfg, idx3, h_tm)
    eo_tm = _ffn(cfg, xg, w_gate, w_up, w_down)
    ffn_tm = _combine(cfg, idx3, gates3, eo_tm)
    return x1, ffn_tm


def _forward(cfg, x, norm_mix, w_in, sink_a, q_norm_b, k_norm_b, w_out, norm_ffn,
             w_router, w_gate, w_up, w_down, norm_final):
    B, S, D = x.shape
    assert norm_mix.shape[0] == 1, "single-layer stack only"
    cos_t, sin_t = _rope_tables(S)
    slopes = jnp.asarray(2.0 ** (-8.0 * np.arange(1, H_A + 1) / H_A), dtype=F32)
    x2 = x.reshape(B * S, D)
    x1, ffn_tm = _layer(cfg, x2, cos_t, sin_t, slopes, norm_mix[0], w_in[0], sink_a[0],
                        q_norm_b[0], k_norm_b[0], w_out[0], norm_ffn[0], w_router[0],
                        w_gate[0], w_up[0], w_down[0])
    out = _final(cfg, ffn_tm, x1, norm_final[None])
    return out.reshape(B, S, D)


def kernel(x, norm_mix, w_in, sink_a, q_norm_b, k_norm_b, w_out, norm_ffn, w_router, w_gate, w_up,
           w_down, norm_final):
    B, S, D = x.shape
    E = w_router.shape[-1]
    cfg = Cfg(batch=B, seq=S, d_model=D, d_ff=w_gate.shape[-1], n_experts=E, cap=2 * S // E)
    return _forward(cfg, x, norm_mix, w_in, sink_a, q_norm_b, k_norm_b, w_out, norm_ffn,
                    w_router, w_gate, w_up, w_down, norm_final)
```

```python
import functools
from typing import NamedTuple

import numpy as np
import jax
import jax.numpy as jnp
from jax import lax
from jax.experimental import pallas as pl
from jax.experimental.pallas import tpu as pltpu

HEAD_DIM = 128
H_A, KV_A = 8, 2
H_B, KV_B = 8, 2
GROUP = 4
WINDOW = 128
BLOCK = 128
GRID_W = 64
ROPE_THETA = 10000.0
EPS = 1e-6
NEG_INF = -1e30
SCALE = HEAD_DIM ** -0.5
LOG2E = 1.4426950408889634
LANES = 128
SUBLANES = 8
N_PROJ_HEADS = H_A + 2 * KV_A + H_B + 2 * KV_B
MIB = 1024 * 1024

F32 = jnp.float32
BF16 = jnp.bfloat16
I32 = jnp.int32


class Cfg(NamedTuple):
    batch: int
    seq: int
    d_model: int
    d_ff: int
    n_experts: int
    cap: int


def _params(sem, vmem_mib):
    return pltpu.CompilerParams(dimension_semantics=sem, vmem_limit_bytes=vmem_mib * MIB)


def _resident(shape, index_map):
    return pl.BlockSpec(shape, index_map, pipeline_mode=pl.Buffered(1))


def _inproj_body(x_ref, g_ref, w_ref, qn_ref, kn_ref, cos_ref, sin_ref,
                 qa_ref, ka_ref, va_ref, qb_ref, kb_ref, vb_ref):
    xf = x_ref[...]
    ms = jnp.mean(xf * xf, axis=-1, keepdims=True)
    h = (xf * lax.rsqrt(ms + EPS) * g_ref[...]).astype(BF16)
    proj = jnp.dot(h, w_ref[...], preferred_element_type=F32)
    cos = cos_ref[...]
    sin = sin_ref[...]
    lane = lax.broadcasted_iota(I32, (1, LANES), 1)
    first = (lane % 64) < 32

    def head(c):
        return proj[:, c * HEAD_DIM:(c + 1) * HEAD_DIM]

    def norm_rope(t, gain):
        m = jnp.mean(t * t, axis=-1, keepdims=True)
        tn = t * lax.rsqrt(m + EPS) * gain
        partner = jnp.where(first, pltpu.roll(tn, 96, 1), pltpu.roll(tn, 32, 1))
        return tn * cos + partner * sin

    c = 0
    for hh in range(H_A):
        qa_ref[hh] = (head(c + hh) * (SCALE * LOG2E)).astype(BF16)
    c += H_A
    for hh in range(KV_A):
        ka_ref[hh] = head(c + hh).astype(BF16)
    c += KV_A
    for hh in range(KV_A):
        va_ref[hh] = head(c + hh).astype(BF16)
    c += KV_A
    for hh in range(H_B):
        qb_ref[hh] = (norm_rope(head(c + hh), qn_ref[...]) * (SCALE * LOG2E)).astype(BF16)
    c += H_B
    for hh in range(KV_B):
        kb_ref[hh] = norm_rope(head(c + hh), kn_ref[...]).astype(BF16)
    c += KV_B
    for hh in range(KV_B):
        vb_ref[hh] = head(c + hh).astype(BF16)


def _inproj(cfg, x2, norm_mix, w_in_bf, q_norm, k_norm, cos_t, sin_t):
    T, D = x2.shape
    tm = 256
    sb = cfg.seq // tm
    heads = lambda n: jax.ShapeDtypeStruct((n, T, HEAD_DIM), BF16)
    hspec = lambda n: pl.BlockSpec((n, tm, HEAD_DIM), lambda i: (0, i, 0))
    return pl.pallas_call(
        _inproj_body,
        grid=(T // tm,),
        in_specs=[
            pl.BlockSpec((tm, D), lambda i: (i, 0)),
            _resident((1, D), lambda i: (0, 0)),
            _resident((D, N_PROJ_HEADS * HEAD_DIM), lambda i: (0, 0)),
            _resident((1, HEAD_DIM), lambda i: (0, 0)),
            _resident((1, HEAD_DIM), lambda i: (0, 0)),
            pl.BlockSpec((tm, HEAD_DIM), lambda i: (i % sb, 0)),
            pl.BlockSpec((tm, HEAD_DIM), lambda i: (i % sb, 0)),
        ],
        out_specs=[hspec(H_A), hspec(KV_A), hspec(KV_A), hspec(H_B), hspec(KV_B), hspec(KV_B)],
        out_shape=[heads(H_A), heads(KV_A), heads(KV_A), heads(H_B), heads(KV_B), heads(KV_B)],
        compiler_params=_params(("parallel",), 40),
        name="inproj",
    )(x2, norm_mix, w_in_bf, q_norm, k_norm, cos_t, sin_t)


def _win_body(sink_ref, slope_ref, q_ref, k_ref, v_ref, o_ref, kt_scr, vx_scr, bias_scr, sink_scr,
              *, seq, unroll):
    kvh = pl.program_id(1)
    span = 3 * BLOCK
    rows = GROUP * BLOCK
    nb = seq // BLOCK

    kt_scr[...] = k_ref[0].astype(F32).T.astype(BF16)
    vx_scr[:, :HEAD_DIM] = v_ref[0]
    vx_scr[:, HEAD_DIM:] = jnp.ones((seq, HEAD_DIM), BF16)
    qi = lax.broadcasted_iota(I32, (BLOCK, span), 0)
    kj = lax.broadcasted_iota(I32, (BLOCK, span), 1)
    for g in range(GROUP):
        hq = kvh * GROUP + g
        sink_scr[g * BLOCK:(g + 1) * BLOCK, :] = jnp.full((BLOCK, LANES), sink_ref[hq], F32) * LOG2E
        for place in range(3):
            dist = jnp.abs(qi + place * BLOCK - kj)
            bias = jnp.where(dist <= WINDOW, (slope_ref[hq] * dist.astype(F32)) * (-LOG2E), NEG_INF)
            bias_scr[place, g * BLOCK:(g + 1) * BLOCK, :] = bias

    def block(n):
        ws = pl.multiple_of(jnp.clip((n - 1) * BLOCK, 0, seq - span), BLOCK)
        q0 = pl.multiple_of(n * BLOCK, BLOCK)
        place = jnp.where(n == 0, 0, jnp.where(n == nb - 1, 2, 1))
        q = q_ref[:, pl.ds(q0, BLOCK), :].reshape(rows, HEAD_DIM)
        s = jnp.dot(q, kt_scr[:, pl.ds(ws, span)], preferred_element_type=F32) + bias_scr[place]
        sink = sink_scr[:, :1]
        m = jnp.maximum(jnp.max(s, axis=-1, keepdims=True), sink)
        p = jnp.exp2(s - m).astype(BF16)
        ox = jnp.dot(p, vx_scr[pl.ds(ws, span), :], preferred_element_type=F32)
        o = ox[:, :HEAD_DIM] / (ox[:, HEAD_DIM:] + jnp.exp2(sink - m))
        for g in range(GROUP):
            o_ref[pl.ds(q0, BLOCK), g * HEAD_DIM:(g + 1) * HEAD_DIM] = (
                o[g * BLOCK:(g + 1) * BLOCK].astype(BF16))

    def step(i, carry):
        for u in range(unroll):
            block(i * unroll + u)
        return carry

    lax.fori_loop(0, nb // unroll, step, 0)


def _win_attention(cfg, sink, slopes, qa, ka, va):
    S = cfg.seq
    T = cfg.batch * S
    assert S // BLOCK >= 4
    smem = pl.BlockSpec(memory_space=pltpu.SMEM)
    rows = GROUP * BLOCK
    return pl.pallas_call(
        functools.partial(_win_body, seq=S, unroll=2),
        grid=(cfg.batch, KV_A),
        in_specs=[
            smem, smem,
            pl.BlockSpec((GROUP, S, HEAD_DIM), lambda b, k: (k, b, 0)),
            pl.BlockSpec((1, S, HEAD_DIM), lambda b, k: (k, b, 0)),
            pl.BlockSpec((1, S, HEAD_DIM), lambda b, k: (k, b, 0)),
        ],
        out_specs=pl.BlockSpec((S, GROUP * HEAD_DIM), lambda b, k: (b, k)),
        out_shape=jax.ShapeDtypeStruct((T, H_A * HEAD_DIM), BF16),
        scratch_shapes=[pltpu.VMEM((HEAD_DIM, S), BF16), pltpu.VMEM((S, 2 * HEAD_DIM), BF16),
                        pltpu.VMEM((3, rows, 3 * BLOCK), F32), pltpu.VMEM((rows, LANES), F32)],
        compiler_params=_params(("parallel", "parallel"), 40),
        name="win_attention",
    )(sink, slopes, qa, ka, va)


def _grid_body(q_ref, k_ref, v_ref, o_ref, kt_scr, vx_scr, *, seq, tq):
    kt_scr[...] = k_ref[0].astype(F32).T.astype(BF16)
    vx_scr[:, :HEAD_DIM] = v_ref[0]
    vx_scr[:, HEAD_DIM:] = jnp.ones((seq, HEAD_DIM), BF16)

    def tile(g, r0):
        q = q_ref[g, pl.ds(r0, tq), :]
        s = jnp.dot(q, kt_scr[...], preferred_element_type=F32)
        m = jnp.max(s, axis=-1, keepdims=True)
        p = jnp.exp2(s - m).astype(BF16)
        ox = jnp.dot(p, vx_scr[...], preferred_element_type=F32)
        o = ox[:, :HEAD_DIM] / ox[:, HEAD_DIM:]
        o_ref[pl.ds(r0, tq), g * HEAD_DIM:(g + 1) * HEAD_DIM] = o.astype(BF16)

    def step(i, carry):
        r0 = pl.multiple_of(i * tq, tq)
        for g in range(GROUP):
            tile(g, r0)
        return carry

    lax.fori_loop(0, seq // tq, step, 0)


def _grid_attention(cfg, qb, kb, vb):
    S = cfg.seq
    T = cfg.batch * S
    tq = 256
    return pl.pallas_call(
        functools.partial(_grid_body, seq=S, tq=tq),
        grid=(cfg.batch, KV_B),
        in_specs=[
            pl.BlockSpec((GROUP, S, HEAD_DIM), lambda b, k: (k, b, 0)),
            pl.BlockSpec((1, S, HEAD_DIM), lambda b, k: (k, b, 0)),
            pl.BlockSpec((1, S, HEAD_DIM), lambda b, k: (k, b, 0)),
        ],
        out_specs=pl.BlockSpec((S, GROUP * HEAD_DIM), lambda b, k: (b, k)),
        out_shape=jax.ShapeDtypeStruct((T, H_B * HEAD_DIM), BF16),
        scratch_shapes=[pltpu.VMEM((HEAD_DIM, S), BF16), pltpu.VMEM((S, 2 * HEAD_DIM), BF16)],
        compiler_params=_params(("parallel", "parallel"), 48),
        name="grid_attention",
    )(qb, kb, vb)


def _outproj_body(oa_ref, ob_ref, x_ref, wa_ref, wb_ref, g_ref, wr_ref,
                  x1_ref, h_ref, aff_ref, *, tm, half_tiles, n_exp):
    mix = jnp.dot(oa_ref[...], wa_ref[...], preferred_element_type=F32)
    mix = mix + jnp.dot(ob_ref[...], wb_ref[...], preferred_element_type=F32)
    x1 = x_ref[...] + mix
    x1_ref[...] = x1
    ms = jnp.mean(x1 * x1, axis=-1, keepdims=True)
    h = x1 * lax.rsqrt(ms + EPS) * g_ref[...]
    h_hi = h.astype(BF16)
    h_lo = (h - h_hi.astype(F32)).astype(BF16)
    r = (jnp.dot(h_hi, wr_ref[...], preferred_element_type=F32)
         + jnp.dot(h_lo, wr_ref[...], preferred_element_type=F32))
    logits = r[:, :n_exp] + r[:, n_exp:]
    e = jnp.exp(logits - jnp.max(logits, axis=-1, keepdims=True))
    aff_ref[...] = e / jnp.sum(e, axis=-1, keepdims=True)
    for half in range(2):
        for j in range(half_tiles):
            c0 = (half * half_tiles + j) * LANES
            h_ref[0, half, pl.ds(j, tm, stride=half_tiles), :] = h[:, c0:c0 + LANES]


def _outproj(cfg, oa, ob, x2, w_a, w_b, norm_ffn, w_router_parts):
    T, D = x2.shape
    E = cfg.n_experts
    S = cfg.seq
    tm = 256
    sb = S // tm
    ht = D // (2 * LANES)
    return pl.pallas_call(
        functools.partial(_outproj_body, tm=tm, half_tiles=ht, n_exp=E),
        grid=(T // tm,),
        in_specs=[
            pl.BlockSpec((tm, oa.shape[1]), lambda i: (i, 0)),
            pl.BlockSpec((tm, ob.shape[1]), lambda i: (i, 0)),
            pl.BlockSpec((tm, D), lambda i: (i, 0)),
            _resident(w_a.shape, lambda i: (0, 0)),
            _resident(w_b.shape, lambda i: (0, 0)),
            _resident((1, D), lambda i: (0, 0)),
            _resident((D, 2 * E), lambda i: (0, 0)),
        ],
        out_specs=[
            pl.BlockSpec((tm, D), lambda i: (i, 0)),
            pl.BlockSpec((1, 2, tm * ht, LANES), lambda i: (i // sb, 0, i % sb, 0)),
            pl.BlockSpec((tm, E), lambda i: (i, 0)),
        ],
        out_shape=[
            jax.ShapeDtypeStruct((T, D), F32),
            jax.ShapeDtypeStruct((cfg.batch, 2, S * ht, LANES), F32),
            jax.ShapeDtypeStruct((T, E), F32),
        ],
        compiler_params=_params(("parallel",), 40),
        name="outproj_router",
    )(oa, ob, x2, w_a, w_b, norm_ffn, w_router_parts)


def _topk_body(aff_ref, idx_ref, c_scr, *, seq, cap, n_exp):
    a = aff_ref[...]
    bits = lax.bitcast_convert_type(a, I32)

    def bisect(i, lo):
        cand = lo | jnp.left_shift(jnp.int32(1), 30 - i)
        cnt = jnp.sum((bits >= cand).astype(I32), axis=0, keepdims=True)
        return jnp.where(cnt >= cap, cand, lo)

    thr = lax.fori_loop(0, 31, bisect, jnp.zeros((1, n_exp), I32))
    gt = bits > thr
    eq = bits == thr
    need = (cap - jnp.sum(gt.astype(I32), axis=0, keepdims=True)).astype(F32)

    rc = 256
    r_i = lax.broadcasted_iota(I32, (rc, rc), 0)
    c_i = lax.broadcasted_iota(I32, (rc, rc), 1)
    lower = (c_i <= r_i).astype(BF16)

    def prefix(mask_f32):
        carry = jnp.zeros((1, n_exp), F32)
        for k in range(seq // rc):
            blk = mask_f32[k * rc:(k + 1) * rc, :]
            inc = jnp.dot(lower, blk.astype(BF16), preferred_element_type=F32) + carry
            c_scr[k * rc:(k + 1) * rc, :] = inc
            carry = inc[rc - 1:rc, :]
        return c_scr[...]

    eq_f = eq.astype(F32)
    tie_rank = prefix(eq_f) - eq_f
    sel = jnp.logical_or(gt, jnp.logical_and(eq, tie_rank < need))
    prefix(sel.astype(F32))

    r_row = lax.broadcasted_iota(I32, (1, cap), 1).astype(F32)
    rows = 512
    for e in range(n_exp):
        def count(k, acc, e=e):
            r0 = pl.multiple_of(k * rows, rows)
            col = c_scr[pl.ds(r0, rows), e:e + 1]
            hit = (col <= r_row).astype(F32)
            return acc + jnp.sum(hit.reshape(rows // SUBLANES, SUBLANES, cap), axis=0)

        acc = lax.fori_loop(0, seq // rows, count, jnp.zeros((SUBLANES, cap), F32))
        idx_ref[0, e:e + 1, :] = jnp.sum(acc, axis=0, keepdims=True).astype(I32)


def _topk(cfg, aff):
    S, E, C = cfg.seq, cfg.n_experts, cfg.cap
    return pl.pallas_call(
        functools.partial(_topk_body, seq=S, cap=C, n_exp=E),
        grid=(cfg.batch,),
        in_specs=[pl.BlockSpec((S, E), lambda b: (b, 0))],
        out_specs=pl.BlockSpec((1, E, C), lambda b: (b, 0, 0)),
        out_shape=jax.ShapeDtypeStruct((cfg.batch, E, C), I32),
        scratch_shapes=[pltpu.VMEM((S, E), F32)],
        compiler_params=_params(("parallel",), 40),
        name="topk",
    )(aff)


def _gather_body(idx_ref, h_ref, o_ref, tile_scr, *, cap, half_tiles, stride):
    for m in range(cap):
        t = idx_ref[0, 0, m]
        row = pl.multiple_of(t * half_tiles, half_tiles)
        tile_scr[pl.ds(m, half_tiles, stride=stride), :] = h_ref[0, 0, pl.ds(row, half_tiles), :]
    for j in range(half_tiles):
        o_ref[0, :, j * LANES:(j + 1) * LANES] = tile_scr[pl.ds(j * stride, cap), :].astype(BF16)


def _gather(cfg, idx3, h_tm):
    B, S, E, C, D = cfg.batch, cfg.seq, cfg.n_experts, cfg.cap, cfg.d_model
    ht = D // (2 * LANES)
    stride = C + SUBLANES
    return pl.pallas_call(
        functools.partial(_gather_body, cap=C, half_tiles=ht, stride=stride),
        grid=(B, 2, E),
        in_specs=[
            pl.BlockSpec((1, 1, C), lambda b, hf, e: (b * E + e, 0, 0), memory_space=pltpu.SMEM),
            _resident((1, 1, S * ht, LANES), lambda b, hf, e: (b, hf, 0, 0)),
        ],
        out_specs=pl.BlockSpec((1, C, D // 2), lambda b, hf, e: (e, b, hf)),
        out_shape=jax.ShapeDtypeStruct((E, B * C, D), BF16),
        scratch_shapes=[pltpu.VMEM((ht * stride, LANES), F32)],
        compiler_params=_params(("parallel", "parallel", "arbitrary"), 40),
        name="gather_tokens",
    )(idx3, h_tm)


def _ffn_body(x_ref, wg_ref, wu_ref, wd_ref, o_ref, acc_scr, *, tm, tiles, n_f):
    f = pl.program_id(2)
    x = x_ref[0]
    a = jnp.dot(x, wg_ref[0].astype(BF16), preferred_element_type=F32)
    u = jnp.dot(x, wu_ref[0].astype(BF16), preferred_element_type=F32)
    hid = (a * jax.nn.sigmoid(a) * u).astype(BF16)
    y = jnp.dot(hid, wd_ref[0].astype(BF16), preferred_element_type=F32)

    @pl.when(f == 0)
    def _():
        acc_scr[...] = y

    @pl.when(f > 0)
    def _():
        acc_scr[...] += y

    @pl.when(f == n_f - 1)
    def _():
        for j in range(tiles):
            o_ref[0, pl.ds(j, tm, stride=tiles), :] = acc_scr[:, j * LANES:(j + 1) * LANES]


def _ffn(cfg, xg, w_gate, w_up, w_down):
    E, D, F = cfg.n_experts, cfg.d_model, cfg.d_ff
    M = cfg.batch * cfg.cap
    tm = min(1024, M)
    tf = min(256, F)
    tiles = D // LANES
    n_f = F // tf
    return pl.pallas_call(
        functools.partial(_ffn_body, tm=tm, tiles=tiles, n_f=n_f),
        grid=(E, M // tm, n_f),
        in_specs=[
            pl.BlockSpec((1, tm, D), lambda e, m, f: (e, m, 0)),
            pl.BlockSpec((1, D, tf), lambda e, m, f: (e, 0, f)),
            pl.BlockSpec((1, D, tf), lambda e, m, f: (e, 0, f)),
            pl.BlockSpec((1, tf, D), lambda e, m, f: (e, f, 0)),
        ],
        out_specs=pl.BlockSpec((1, tm * tiles, LANES), lambda e, m, f: (e, m, 0)),
        out_shape=jax.ShapeDtypeStruct((E, M * tiles, LANES), F32),
        scratch_shapes=[pltpu.VMEM((tm, D), F32)],
        compiler_params=_params(("parallel", "parallel", "arbitrary"), 52),
        name="expert_ffn",
    )(xg, w_gate, w_up, w_down)


def _combine_body(idx_ref, gate_ref, eo_ref, o_ref, *, cap, tiles, th, steps):
    tok0 = pl.program_id(1) * th
    e = pl.program_id(2)

    @pl.when(e == 0)
    def _():
        o_ref[...] = jnp.zeros(o_ref.shape, F32)

    def lower_bound(v):
        def step(_, lh):
            lo, hi = lh
            mid = jnp.minimum((lo + hi) // 2, cap - 1)
            below = idx_ref[0, 0, mid] < v
            return jnp.where(below, mid + 1, lo), jnp.where(below, hi, mid)

        return lax.fori_loop(0, steps, step, (jnp.int32(0), jnp.int32(cap)))[0]

    first = lower_bound(tok0)
    last = lower_bound(tok0 + th)

    def add_row(m, carry):
        t = idx_ref[0, 0, m]
        gate = gate_ref[0, 0, t]
        dst = pl.multiple_of((t - tok0) * tiles, tiles)
        src = pl.multiple_of(m * tiles, tiles)
        o_ref[0, pl.ds(dst, tiles), :] = (o_ref[0, pl.ds(dst, tiles), :]
                                          + eo_ref[0, pl.ds(src, tiles), :] * gate)
        return carry

    lax.fori_loop(first, last, add_row, 0)


def _combine(cfg, idx3, gates3, eo_tm):
    B, S, E, C, D = cfg.batch, cfg.seq, cfg.n_experts, cfg.cap, cfg.d_model
    tiles = D // LANES
    th = min(2048, S)
    steps = int(np.ceil(np.log2(C))) + 1
    return pl.pallas_call(
        functools.partial(_combine_body, cap=C, tiles=tiles, th=th, steps=steps),
        grid=(B, S // th, E),
        in_specs=[
            pl.BlockSpec((1, 1, C), lambda b, h, e: (b * E + e, 0, 0), memory_space=pltpu.SMEM),
            pl.BlockSpec((1, 1, S), lambda b, h, e: (e * B + b, 0, 0), memory_space=pltpu.SMEM),
            pl.BlockSpec((1, C * tiles, LANES), lambda b, h, e: (e, b, 0)),
        ],
        out_specs=pl.BlockSpec((1, th * tiles, LANES), lambda b, h, e: (b, h, 0)),
        out_shape=jax.ShapeDtypeStruct((B, S * tiles, LANES), F32),
        compiler_params=_params(("parallel", "parallel", "arbitrary"), 48),
        name="combine",
    )(idx3, gates3, eo_tm)


def _final_body(f_ref, x_ref, g_ref, o_ref, *, tm, tiles, d_model):
    ss = jnp.zeros((tm, 1), F32)
    for j in range(tiles):
        cols = slice(j * LANES, (j + 1) * LANES)
        y = x_ref[:, cols] + f_ref[0, pl.ds(j, tm, stride=tiles), :]
        o_ref[:, cols] = y
        ss = ss + jnp.sum(y * y, axis=-1, keepdims=True)
    o_ref[...] = o_ref[...] * lax.rsqrt(ss / d_model + EPS) * g_ref[...]


def _final(cfg, ffn_tm, x1, norm_final):
    T, D = x1.shape
    tiles = D // LANES
    tm = 256
    sb = cfg.seq // tm
    return pl.pallas_call(
        functools.partial(_final_body, tm=tm, tiles=tiles, d_model=D),
        grid=(T // tm,),
        in_specs=[
            pl.BlockSpec((1, tm * tiles, LANES), lambda i: (i // sb, i % sb, 0)),
            pl.BlockSpec((tm, D), lambda i: (i, 0)),
            _resident((1, D), lambda i: (0, 0)),
        ],
        out_specs=pl.BlockSpec((tm, D), lambda i: (i, 0)),
        out_shape=jax.ShapeDtypeStruct((T, D), F32),
        compiler_params=_params(("parallel",), 40),
        name="final_norm",
    )(ffn_tm, x1, norm_final)


def _rope_tables(seq):
    rows = seq // GRID_W
    row = jnp.broadcast_to(jnp.arange(rows)[:, None], (rows, GRID_W)).reshape(seq)
    col = jnp.broadcast_to(jnp.arange(GRID_W)[None, :], (rows, GRID_W)).reshape(seq)
    half = HEAD_DIM // 2
    inv_freq = ROPE_THETA ** (-jnp.arange(0, half, 2, dtype=F32) / half)
    ang_r = row.astype(F32)[:, None] * inv_freq[None, :]
    ang_c = col.astype(F32)[:, None] * inv_freq[None, :]
    cos_t = jnp.concatenate([jnp.cos(ang_r), jnp.cos(ang_r), jnp.cos(ang_c), jnp.cos(ang_c)], axis=-1)
    sin_t = jnp.concatenate([-jnp.sin(ang_r), jnp.sin(ang_r), -jnp.sin(ang_c), jnp.sin(ang_c)], axis=-1)
    return cos_t, sin_t


def _layer(cfg, x2, cos_t, sin_t, slopes, norm_mix, w_in, sink_a, q_norm_b, k_norm_b, w_out,
           norm_ffn, w_router, w_gate, w_up, w_down):
    B, S, E, C = cfg.batch, cfg.seq, cfg.n_experts, cfg.cap
    qa, ka, va, qb, kb, vb = _inproj(cfg, x2, norm_mix[None], w_in.astype(BF16),
                                     q_norm_b[None], k_norm_b[None], cos_t, sin_t)
    oa = _win_attention(cfg, sink_a, slopes, qa, ka, va)
    ob = _grid_attention(cfg, qb, kb, vb)
    w_out_bf = w_out.astype(BF16)
    n_a = H_A * HEAD_DIM
    wr_hi = w_router.astype(BF16)
    wr_lo = (w_router - wr_hi.astype(F32)).astype(BF16)
    x1, h_tm, aff = _outproj(cfg, oa, ob, x2, w_out_bf[:n_a], w_out_bf[n_a:], norm_ffn[None],
                             jnp.concatenate([wr_hi, wr_lo], axis=-1))
    idx = _topk(cfg, aff)
    idx3 = idx.reshape(B * E, 1, C)
    gates3 = aff.T.reshape(E * B, 1, S)
    xg = _gather(cfg, idx3, h_tm)
    eo_tm = _ffn(cfg, xg, w_gate, w_up, w_down)
    ffn_tm = _combine(cfg, idx3, gates3, eo_tm)
    return x1, ffn_tm


def _forward(cfg, x, norm_mix, w_in, sink_a, q_norm_b, k_norm_b, w_out, norm_ffn,
             w_router, w_gate, w_up, w_down, norm_final):
    B, S, D = x.shape
    assert norm_mix.shape[0] == 1, "single-layer stack only"
    cos_t, sin_t = _rope_tables(S)
    slopes = jnp.asarray(2.0 ** (-8.0 * np.arange(1, H_A + 1) / H_A), dtype=F32)
    x2 = x.reshape(B * S, D)
    x1, ffn_tm = _layer(cfg, x2, cos_t, sin_t, slopes, norm_mix[0], w_in[0], sink_a[0],
                        q_norm_b[0], k_norm_b[0], w_out[0], norm_ffn[0], w_router[0],
                        w_gate[0], w_up[0], w_down[0])
    out = _final(cfg, ffn_tm, x1, norm_final[None])
    return out.reshape(B, S, D)


def kernel(x, norm_mix, w_in, sink_a, q_norm_b, k_norm_b, w_out, norm_ffn, w_router, w_gate, w_up,
           w_down, norm_final):
    B, S, D = x.shape
    E = w_router.shape[-1]
    cfg = Cfg(batch=B, seq=S, d_model=D, d_ff=w_gate.shape[-1], n_experts=E, cap=2 * S // E)
    return _forward(cfg, x, norm_mix, w_in, sink_a, q_norm_b, k_norm_b, w_out, norm_ffn,
                    w_router, w_gate, w_up, w_down, norm_final)
```

```python
import functools
from typing import NamedTuple

import numpy as np
import jax
import jax.numpy as jnp
from jax import lax
from jax.experimental import pallas as pl
from jax.experimental.pallas import tpu as pltpu

HEAD_DIM = 128
H_A, KV_A = 8, 2
H_B, KV_B = 8, 2
GROUP = 4
WINDOW = 128
BLOCK = 128
GRID_W = 64
ROPE_THETA = 10000.0
EPS = 1e-6
NEG_INF = -1e30
SCALE = HEAD_DIM ** -0.5
LOG2E = 1.4426950408889634
LANES = 128
SUBLANES = 8
N_PROJ_HEADS = H_A + 2 * KV_A + H_B + 2 * KV_B
MIB = 1024 * 1024

F32 = jnp.float32
BF16 = jnp.bfloat16
I32 = jnp.int32


class Cfg(NamedTuple):
    batch: int
    seq: int
    d_model: int
    d_ff: int
    n_experts: int
    cap: int


def _params(sem, vmem_mib):
    return pltpu.CompilerParams(dimension_semantics=sem, vmem_limit_bytes=vmem_mib * MIB)


def _resident(shape, index_map):
    return pl.BlockSpec(shape, index_map, pipeline_mode=pl.Buffered(1))


def _inproj_body(x_ref, g_ref, w_ref, qn_ref, kn_ref, cos_ref, sin_ref,
                 qa_ref, ka_ref, va_ref, qb_ref, kb_ref, vb_ref):
    xf = x_ref[...]
    ms = jnp.mean(xf * xf, axis=-1, keepdims=True)
    h = (xf * lax.rsqrt(ms + EPS) * g_ref[...]).astype(BF16)
    proj = jnp.dot(h, w_ref[...], preferred_element_type=F32)
    cos = cos_ref[...]
    sin = sin_ref[...]
    lane = lax.broadcasted_iota(I32, (1, LANES), 1)
    first = (lane % 64) < 32

    def head(c):
        return proj[:, c * HEAD_DIM:(c + 1) * HEAD_DIM]

    def norm_rope(t, gain):
        m = jnp.mean(t * t, axis=-1, keepdims=True)
        tn = t * lax.rsqrt(m + EPS) * gain
        partner = jnp.where(first, pltpu.roll(tn, 96, 1), pltpu.roll(tn, 32, 1))
        return tn * cos + partner * sin

    c = 0
    for hh in range(H_A):
        qa_ref[hh] = (head(c + hh) * (SCALE * LOG2E)).astype(BF16)
    c += H_A
    for hh in range(KV_A):
        ka_ref[hh] = head(c + hh).astype(BF16)
    c += KV_A
    for hh in range(KV_A):
        va_ref[hh] = head(c + hh).astype(BF16)
    c += KV_A
    for hh in range(H_B):
        qb_ref[hh] = (norm_rope(head(c + hh), qn_ref[...]) * (SCALE * LOG2E)).astype(BF16)
    c += H_B
    for hh in range(KV_B):
        kb_ref[hh] = norm_rope(head(c + hh), kn_ref[...]).astype(BF16)
    c += KV_B
    for hh in range(KV_B):
        vb_ref[hh] = head(c + hh).astype(BF16)


def _inproj(cfg, x2, norm_mix, w_in_bf, q_norm, k_norm, cos_t, sin_t):
    T, D = x2.shape
    tm = 256
    sb = cfg.seq // tm
    heads = lambda n: jax.ShapeDtypeStruct((n, T, HEAD_DIM), BF16)
    hspec = lambda n: pl.BlockSpec((n, tm, HEAD_DIM), lambda i: (0, i, 0))
    return pl.pallas_call(
        _inproj_body,
        grid=(T // tm,),
        in_specs=[
            pl.BlockSpec((tm, D), lambda i: (i, 0)),
            _resident((1, D), lambda i: (0, 0)),
            _resident((D, N_PROJ_HEADS * HEAD_DIM), lambda i: (0, 0)),
            _resident((1, HEAD_DIM), lambda i: (0, 0)),
            _resident((1, HEAD_DIM), lambda i: (0, 0)),
            pl.BlockSpec((tm, HEAD_DIM), lambda i: (i % sb, 0)),
            pl.BlockSpec((tm, HEAD_DIM), lambda i: (i % sb, 0)),
        ],
        out_specs=[hspec(H_A), hspec(KV_A), hspec(KV_A), hspec(H_B), hspec(KV_B), hspec(KV_B)],
        out_shape=[heads(H_A), heads(KV_A), heads(KV_A), heads(H_B), heads(KV_B), heads(KV_B)],
        compiler_params=_params(("parallel",), 40),
        name="inproj",
    )(x2, norm_mix, w_in_bf, q_norm, k_norm, cos_t, sin_t)


def _win_body(sink_ref, slope_ref, q_ref, k_ref, v_ref, o_ref, kt_scr, vx_scr, bias_scr, sink_scr,
              *, seq, unroll):
    kvh = pl.program_id(1)
    span = 3 * BLOCK
    rows = GROUP * BLOCK
    nb = seq // BLOCK

    kt_scr[...] = k_ref[0].astype(F32).T.astype(BF16)
    vx_scr[:, :HEAD_DIM] = v_ref[0]
    vx_scr[:, HEAD_DIM:] = jnp.ones((seq, HEAD_DIM), BF16)
    qi = lax.broadcasted_iota(I32, (BLOCK, span), 0)
    kj = lax.broadcasted_iota(I32, (BLOCK, span), 1)
    for g in range(GROUP):
        hq = kvh * GROUP + g
        sink_scr[g * BLOCK:(g + 1) * BLOCK, :] = jnp.full((BLOCK, LANES), sink_ref[hq], F32) * LOG2E
        for place in range(3):
            dist = jnp.abs(qi + place * BLOCK - kj)
            bias = jnp.where(dist <= WINDOW, (slope_ref[hq] * dist.astype(F32)) * (-LOG2E), NEG_INF)
            bias_scr[place, g * BLOCK:(g + 1) * BLOCK, :] = bias

    def block(n):
        ws = pl.multiple_of(jnp.clip((n - 1) * BLOCK, 0, seq - span), BLOCK)
        q0 = pl.multiple_of(n * BLOCK, BLOCK)
        place = jnp.where(n == 0, 0, jnp.where(n == nb - 1, 2, 1))
        q = q_ref[:, pl.ds(q0, BLOCK), :].reshape(rows, HEAD_DIM)
        s = jnp.dot(q, kt_scr[:, pl.ds(ws, span)], preferred_element_type=F32) + bias_scr[place]
        sink = sink_scr[:, :1]
        m = jnp.maximum(jnp.max(s, axis=-1, keepdims=True), sink)
        p = jnp.exp2(s - m).astype(BF16)
        ox = jnp.dot(p, vx_scr[pl.ds(ws, span), :], preferred_element_type=F32)
        o = ox[:, :HEAD_DIM] / (ox[:, HEAD_DIM:] + jnp.exp2(sink - m))
        for g in range(GROUP):
            o_ref[pl.ds(q0, BLOCK), g * HEAD_DIM:(g + 1) * HEAD_DIM] = (
                o[g * BLOCK:(g + 1) * BLOCK].astype(BF16))

    def step(i, carry):
        for u in range(unroll):
            block(i * unroll + u)
        return carry

    lax.fori_loop(0, nb // unroll, step, 0)


def _win_attention(cfg, sink, slopes, qa, ka, va):
    S = cfg.seq
    T = cfg.batch * S
    assert S // BLOCK >= 4
    smem = pl.BlockSpec(memory_space=pltpu.SMEM)
    rows = GROUP * BLOCK
    return pl.pallas_call(
        functools.partial(_win_body, seq=S, unroll=2),
        grid=(cfg.batch, KV_A),
        in_specs=[
            smem, smem,
            pl.BlockSpec((GROUP, S, HEAD_DIM), lambda b, k: (k, b, 0)),
            pl.BlockSpec((1, S, HEAD_DIM), lambda b, k: (k, b, 0)),
            pl.BlockSpec((1, S, HEAD_DIM), lambda b, k: (k, b, 0)),
        ],
        out_specs=pl.BlockSpec((S, GROUP * HEAD_DIM), lambda b, k: (b, k)),
        out_shape=jax.ShapeDtypeStruct((T, H_A * HEAD_DIM), BF16),
        scratch_shapes=[pltpu.VMEM((HEAD_DIM, S), BF16), pltpu.VMEM((S, 2 * HEAD_DIM), BF16),
                        pltpu.VMEM((3, rows, 3 * BLOCK), F32), pltpu.VMEM((rows, LANES), F32)],
        compiler_params=_params(("parallel", "parallel"), 40),
        name="win_attention",
    )(sink, slopes, qa, ka, va)


def _grid_body(q_ref, k_ref, v_ref, o_ref, kt_scr, vx_scr, *, seq, tq):
    kt_scr[...] = k_ref[0].astype(F32).T.astype(BF16)
    vx_scr[:, :HEAD_DIM] = v_ref[0]
    vx_scr[:, HEAD_DIM:] = jnp.ones((seq, HEAD_DIM), BF16)

    def tile(g, r0):
        q = q_ref[g, pl.ds(r0, tq), :]
        s = jnp.dot(q, kt_scr[...], preferred_element_type=F32)
        m = jnp.max(s, axis=-1, keepdims=True)
        p = jnp.exp2(s - m).astype(BF16)
        ox = jnp.dot(p, vx_scr[...], preferred_element_type=F32)
        o = ox[:, :HEAD_DIM] / ox[:, HEAD_DIM:]
        o_ref[pl.ds(r0, tq), g * HEAD_DIM:(g + 1) * HEAD_DIM] = o.astype(BF16)

    def step(i, carry):
        r0 = pl.multiple_of(i * tq, tq)
        for g in range(GROUP):
            tile(g, r0)
        return carry

    lax.fori_loop(0, seq // tq, step, 0)


def _grid_attention(cfg, qb, kb, vb):
    S = cfg.seq
    T = cfg.batch * S
    tq = 256
    return pl.pallas_call(
        functools.partial(_grid_body, seq=S, tq=tq),
        grid=(cfg.batch, KV_B),
        in_specs=[
            pl.BlockSpec((GROUP, S, HEAD_DIM), lambda b, k: (k, b, 0)),
            pl.BlockSpec((1, S, HEAD_DIM), lambda b, k: (k, b, 0)),
            pl.BlockSpec((1, S, HEAD_DIM), lambda b, k: (k, b, 0)),
        ],
        out_specs=pl.BlockSpec((S, GROUP * HEAD_DIM), lambda b, k: (b, k)),
        out_shape=jax.ShapeDtypeStruct((T, H_B * HEAD_DIM), BF16),
        scratch_shapes=[pltpu.VMEM((HEAD_DIM, S), BF16), pltpu.VMEM((S, 2 * HEAD_DIM), BF16)],
        compiler_params=_params(("parallel", "parallel"), 48),
        name="grid_attention",
    )(qb, kb, vb)


def _outproj_body(oa_ref, ob_ref, x_ref, wa_ref, wb_ref, g_ref, wr_ref,
                  x1_ref, h_ref, aff_ref, *, tm, half_tiles, n_exp):
    mix = jnp.dot(oa_ref[...], wa_ref[...], preferred_element_type=F32)
    mix = mix + jnp.dot(ob_ref[...], wb_ref[...], preferred_element_type=F32)
    x1 = x_ref[...] + mix
    x1_ref[...] = x1
    ms = jnp.mean(x1 * x1, axis=-1, keepdims=True)
    h = x1 * lax.rsqrt(ms + EPS) * g_ref[...]
    h_hi = h.astype(BF16)
    h_lo = (h - h_hi.astype(F32)).astype(BF16)
    r = (jnp.dot(h_hi, wr_ref[...], preferred_element_type=F32)
         + jnp.dot(h_lo, wr_ref[...], preferred_element_type=F32))
    logits = r[:, :n_exp] + r[:, n_exp:]
    e = jnp.exp(logits - jnp.max(logits, axis=-1, keepdims=True))
    aff_ref[...] = e / jnp.sum(e, axis=-1, keepdims=True)
    for half in range(2):
        for j in range(half_tiles):
            c0 = (half * half_tiles + j) * LANES
            h_ref[0, half, pl.ds(j, tm, stride=half_tiles), :] = h[:, c0:c0 + LANES]


def _outproj(cfg, oa, ob, x2, w_a, w_b, norm_ffn, w_router_parts):
    T, D = x2.shape
    E = cfg.n_experts
    S = cfg.seq
    tm = 256
    sb = S // tm
    ht = D // (2 * LANES)
    return pl.pallas_call(
        functools.partial(_outproj_body, tm=tm, half_tiles=ht, n_exp=E),
        grid=(T // tm,),
        in_specs=[
            pl.BlockSpec((tm, oa.shape[1]), lambda i: (i, 0)),
            pl.BlockSpec((tm, ob.shape[1]), lambda i: (i, 0)),
            pl.BlockSpec((tm, D), lambda i: (i, 0)),
            _resident(w_a.shape, lambda i: (0, 0)),
            _resident(w_b.shape, lambda i: (0, 0)),
            _resident((1, D), lambda i: (0, 0)),
            _resident((D, 2 * E), lambda i: (0, 0)),
        ],
        out_specs=[
            pl.BlockSpec((tm, D), lambda i: (i, 0)),
            pl.BlockSpec((1, 2, tm * ht, LANES), lambda i: (i // sb, 0, i % sb, 0)),
            pl.BlockSpec((tm, E), lambda i: (i, 0)),
        ],
        out_shape=[
            jax.ShapeDtypeStruct((T, D), F32),
            jax.ShapeDtypeStruct((cfg.batch, 2, S * ht, LANES), F32),
            jax.ShapeDtypeStruct((T, E), F32),
        ],
        compiler_params=_params(("parallel",), 40),
        name="outproj_router",
    )(oa, ob, x2, w_a, w_b, norm_ffn, w_router_parts)


def _topk_body(aff_ref, idx_ref, c_scr, *, seq, cap, n_exp):
    a = aff_ref[...]
    bits = lax.bitcast_convert_type(a, I32)

    def bisect(i, lo):
        cand = lo | jnp.left_shift(jnp.int32(1), 30 - i)
        cnt = jnp.sum((bits >= cand).astype(I32), axis=0, keepdims=True)
        return jnp.where(cnt >= cap, cand, lo)

    thr = lax.fori_loop(0, 31, bisect, jnp.zeros((1, n_exp), I32))
    gt = bits > thr
    eq = bits == thr
    need = (cap - jnp.sum(gt.astype(I32), axis=0, keepdims=True)).astype(F32)

    rc = 256
    r_i = lax.broadcasted_iota(I32, (rc, rc), 0)
    c_i = lax.broadcasted_iota(I32, (rc, rc), 1)
    lower = (c_i <= r_i).astype(BF16)

    def prefix(mask_f32):
        carry = jnp.zeros((1, n_exp), F32)
        for k in range(seq // rc):
            blk = mask_f32[k * rc:(k + 1) * rc, :]
            inc = jnp.dot(lower, blk.astype(BF16), preferred_element_type=F32) + carry
            c_scr[k * rc:(k + 1) * rc, :] = inc
            carry = inc[rc - 1:rc, :]
        return c_scr[...]

    eq_f = eq.astype(F32)
    tie_rank = prefix(eq_f) - eq_f
    sel = jnp.logical_or(gt, jnp.logical_and(eq, tie_rank < need))
    prefix(sel.astype(F32))

    r_row = lax.broadcasted_iota(I32, (1, cap), 1).astype(F32)
    rows = 512
    for e in range(n_exp):
        def count(k, acc, e=e):
            r0 = pl.multiple_of(k * rows, rows)
            col = c_scr[pl.ds(r0, rows), e:e + 1]
            hit = (col <= r_row).astype(F32)
            return acc + jnp.sum(hit.reshape(rows // SUBLANES, SUBLANES, cap), axis=0)

        acc = lax.fori_loop(0, seq // rows, count, jnp.zeros((SUBLANES, cap), F32))
        idx_ref[0, e:e + 1, :] = jnp.sum(acc, axis=0, keepdims=True).astype(I32)


def _topk(cfg, aff):
    S, E, C = cfg.seq, cfg.n_experts, cfg.cap
    return pl.pallas_call(
        functools.partial(_topk_body, seq=S, cap=C, n_exp=E),
        grid=(cfg.batch,),
        in_specs=[pl.BlockSpec((S, E), lambda b: (b, 0))],
        out_specs=pl.BlockSpec((1, E, C), lambda b: (b, 0, 0)),
        out_shape=jax.ShapeDtypeStruct((cfg.batch, E, C), I32),
        scratch_shapes=[pltpu.VMEM((S, E), F32)],
        compiler_params=_params(("parallel",), 40),
        name="topk",
    )(aff)


def _gather_body(idx_ref, h_ref, o_ref, tile_scr, *, cap, half_tiles, stride):
    for m in range(cap):
        t = idx_ref[0, 0, m]
        row = pl.multiple_of(t * half_tiles, half_tiles)
        tile_scr[pl.ds(m, half_tiles, stride=stride), :] = h_ref[0, 0, pl.ds(row, half_tiles), :]
    for j in range(half_tiles):
        o_ref[0, :, j * LANES:(j + 1) * LANES] = tile_scr[pl.ds(j * stride, cap), :].astype(BF16)


def _gather(cfg, idx3, h_tm):
    B, S, E, C, D = cfg.batch, cfg.seq, cfg.n_experts, cfg.cap, cfg.d_model
    ht = D // (2 * LANES)
    stride = C + SUBLANES
    return pl.pallas_call(
        functools.partial(_gather_body, cap=C, half_tiles=ht, stride=stride),
        grid=(B, 2, E),
        in_specs=[
            pl.BlockSpec((1, 1, C), lambda b, hf, e: (b * E + e, 0, 0), memory_space=pltpu.SMEM),
            _resident((1, 1, S * ht, LANES), lambda b, hf, e: (b, hf, 0, 0)),
        ],
        out_specs=pl.BlockSpec((1, C, D // 2), lambda b, hf, e: (e, b, hf)),
        out_shape=jax.ShapeDtypeStruct((E, B * C, D), BF16),
        scratch_shapes=[pltpu.VMEM((ht * stride, LANES), F32)],
        compiler_params=_params(("parallel", "parallel", "arbitrary"), 40),
        name="gather_tokens",
    )(idx3, h_tm)


def _ffn_body(x_ref, wg_ref, wu_ref, wd_ref, o_ref, h_scr, *, tm, tf, tn, half_tiles, n_f, n_n):
    j = pl.program_id(2)

    @pl.when(j < n_f)
    def _():
        x = x_ref[0]
        a = jnp.dot(x, wg_ref[0].astype(BF16), preferred_element_type=F32)
        u = jnp.dot(x, wu_ref[0].astype(BF16), preferred_element_type=F32)
        hid = (a * jax.nn.sigmoid(a) * u).astype(BF16)
        for f in range(n_f):
            @pl.when(j == f)
            def _(f=f):
                h_scr[:, f * tf:(f + 1) * tf] = hid

    for n in range(n_n):
        @pl.when(j == n_f + n)
        def _(n=n):
            y = jnp.dot(h_scr[...], wd_ref[0].astype(BF16), preferred_element_type=F32)
            for c in range(tn // LANES):
                half, row = divmod(n * (tn // LANES) + c, half_tiles)
                o_ref[0, half, pl.ds(row, tm, stride=half_tiles), :] = y[:, c * LANES:(c + 1) * LANES]


def _ffn(cfg, xg, w_gate, w_up, w_down):
    E, D, F = cfg.n_experts, cfg.d_model, cfg.d_ff
    M = cfg.batch * cfg.cap
    tm = min(1024, M)
    tf = min(256, F)
    tn = 256
    ht = D // (2 * LANES)
    n_f = F // tf
    n_n = D // tn
    return pl.pallas_call(
        functools.partial(_ffn_body, tm=tm, tf=tf, tn=tn, half_tiles=ht, n_f=n_f, n_n=n_n),
        grid=(E, M // tm, n_f + n_n),
        in_specs=[
            _resident((1, tm, D), lambda e, m, j: (e, m, 0)),
            pl.BlockSpec((1, D, tf), lambda e, m, j: (e, 0, jnp.minimum(j, n_f - 1))),
            pl.BlockSpec((1, D, tf), lambda e, m, j: (e, 0, jnp.minimum(j, n_f - 1))),
            pl.BlockSpec((1, F, tn), lambda e, m, j: (e, 0, jnp.maximum(j - n_f, 0))),
        ],
        out_specs=pl.BlockSpec((1, 2, tm * ht, LANES), lambda e, m, j: (e, 0, m, 0)),
        out_shape=jax.ShapeDtypeStruct((E, 2, M * ht, LANES), F32),
        scratch_shapes=[pltpu.VMEM((tm, F), BF16)],
        compiler_params=_params(("parallel", "parallel", "arbitrary"), 52),
        name="expert_ffn",
    )(xg, w_gate, w_up, w_down)


def _combine_body(idx_ref, gate_ref, eo_ref, o_ref, *, cap, half_tiles, unroll):
    e = pl.program_id(2)

    @pl.when(e == 0)
    def _():
        o_ref[...] = jnp.zeros(o_ref.shape, F32)

    def group(g, carry):
        base = g * unroll
        dsts, vals = [], []
        for u in range(unroll):
            t = idx_ref[0, 0, base + u]
            dst = pl.multiple_of(t * half_tiles, half_tiles)
            src = pl.multiple_of((base + u) * half_tiles, half_tiles)
            dsts.append(dst)
            vals.append(o_ref[0, 0, pl.ds(dst, half_tiles), :]
                        + eo_ref[0, 0, pl.ds(src, half_tiles), :] * gate_ref[0, 0, t])
        for u in range(unroll):
            o_ref[0, 0, pl.ds(dsts[u], half_tiles), :] = vals[u]
        return carry

    lax.fori_loop(0, cap // unroll, group, 0)


def _combine(cfg, idx3, gates3, eo_tm):
    B, S, E, C, D = cfg.batch, cfg.seq, cfg.n_experts, cfg.cap, cfg.d_model
    ht = D // (2 * LANES)
    return pl.pallas_call(
        functools.partial(_combine_body, cap=C, half_tiles=ht, unroll=8),
        grid=(B, 2, E),
        in_specs=[
            pl.BlockSpec((1, 1, C), lambda b, h, e: (b * E + e, 0, 0), memory_space=pltpu.SMEM),
            pl.BlockSpec((1, 1, S), lambda b, h, e: (e * B + b, 0, 0), memory_space=pltpu.SMEM),
            pl.BlockSpec((1, 1, C * ht, LANES), lambda b, h, e: (e, h, b, 0)),
        ],
        out_specs=pl.BlockSpec((1, 1, S * ht, LANES), lambda b, h, e: (b, h, 0, 0)),
        out_shape=jax.ShapeDtypeStruct((B, 2, S * ht, LANES), F32),
        compiler_params=_params(("parallel", "parallel", "arbitrary"), 48),
        name="combine",
    )(idx3, gates3, eo_tm)


def _final_body(f_ref, x_ref, g_ref, o_ref, *, tm, half_tiles, d_model):
    ss = jnp.zeros((tm, 1), F32)
    for j in range(2 * half_tiles):
        cols = slice(j * LANES, (j + 1) * LANES)
        y = x_ref[:, cols] + f_ref[0, j // half_tiles, pl.ds(j % half_tiles, tm, stride=half_tiles), :]
        o_ref[:, cols] = y
        ss = ss + jnp.sum(y * y, axis=-1, keepdims=True)
    o_ref[...] = o_ref[...] * lax.rsqrt(ss / d_model + EPS) * g_ref[...]


def _final(cfg, ffn_tm, x1, norm_final):
    T, D = x1.shape
    ht = D // (2 * LANES)
    tm = 256
    sb = cfg.seq // tm
    return pl.pallas_call(
        functools.partial(_final_body, tm=tm, half_tiles=ht, d_model=D),
        grid=(T // tm,),
        in_specs=[
            pl.BlockSpec((1, 2, tm * ht, LANES), lambda i: (i // sb, 0, i % sb, 0)),
            pl.BlockSpec((tm, D), lambda i: (i, 0)),
            _resident((1, D), lambda i: (0, 0)),
        ],
        out_specs=pl.BlockSpec((tm, D), lambda i: (i, 0)),
        out_shape=jax.ShapeDtypeStruct((T, D), F32),
        compiler_params=_params(("parallel",), 40),
        name="final_norm",
    )(ffn_tm, x1, norm_final)


def _rope_tables(seq):
    rows = seq // GRID_W
    row = jnp.broadcast_to(jnp.arange(rows)[:, None], (rows, GRID_W)).reshape(seq)
    col = jnp.broadcast_to(jnp.arange(GRID_W)[None, :], (rows, GRID_W)).reshape(seq)
    half = HEAD_DIM // 2
    inv_freq = ROPE_THETA ** (-jnp.arange(0, half, 2, dtype=F32) / half)
    ang_r = row.astype(F32)[:, None] * inv_freq[None, :]
    ang_c = col.astype(F32)[:, None] * inv_freq[None, :]
    cos_t = jnp.concatenate([jnp.cos(ang_r), jnp.cos(ang_r), jnp.cos(ang_c), jnp.cos(ang_c)], axis=-1)
    sin_t = jnp.concatenate([-jnp.sin(ang_r), jnp.sin(ang_r), -jnp.sin(ang_c), jnp.sin(ang_c)], axis=-1)
    return cos_t, sin_t


def _layer(cfg, x2, cos_t, sin_t, slopes, norm_mix, w_in, sink_a, q_norm_b, k_norm_b, w_out,
           norm_ffn, w_router, w_gate, w_up, w_down):
    B, S, E, C = cfg.batch, cfg.seq, cfg.n_experts, cfg.cap
    qa, ka, va, qb, kb, vb = _inproj(cfg, x2, norm_mix[None], w_in.astype(BF16),
                                     q_norm_b[None], k_norm_b[None], cos_t, sin_t)
    oa = _win_attention(cfg, sink_a, slopes, qa, ka, va)
    ob = _grid_attention(cfg, qb, kb, vb)
    w_out_bf = w_out.astype(BF16)
    n_a = H_A * HEAD_DIM
    wr_hi = w_router.astype(BF16)
    wr_lo = (w_router - wr_hi.astype(F32)).astype(BF16)
    x1, h_tm, aff = _outproj(cfg, oa, ob, x2, w_out_bf[:n_a], w_out_bf[n_a:], norm_ffn[None],
                             jnp.concatenate([wr_hi, wr_lo], axis=-1))
    idx = _topk(cfg, aff)
    idx3 = idx.reshape(B * E, 1, C)
    gates3 = aff.T.reshape(E * B, 1, S)
    xg = _gather(cfg, idx3, h_tm)
    eo_tm = _ffn(cfg, xg, w_gate, w_up, w_down)
    ffn_tm = _combine(cfg, idx3, gates3, eo_tm)
    return x1, ffn_tm


def _forward(cfg, x, norm_mix, w_in, sink_a, q_norm_b, k_norm_b, w_out, norm_ffn,
             w_router, w_gate, w_up, w_down, norm_final):
    B, S, D = x.shape
    assert norm_mix.shape[0] == 1, "single-layer stack only"
    cos_t, sin_t = _rope_tables(S)
    slopes = jnp.asarray(2.0 ** (-8.0 * np.arange(1, H_A + 1) / H_A), dtype=F32)
    x2 = x.reshape(B * S, D)
    x1, ffn_tm = _layer(cfg, x2, cos_t, sin_t, slopes, norm_mix[0], w_in[0], sink_a[0],
                        q_norm_b[0], k_norm_b[0], w_out[0], norm_ffn[0], w_router[0],
                        w_gate[0], w_up[0], w_down[0])
    out = _final(cfg, ffn_tm, x1, norm_final[None])
    return out.reshape(B, S, D)


def kernel(x, norm_mix, w_in, sink_a, q_norm_b, k_norm_b, w_out, norm_ffn, w_router, w_gate, w_up,
           w_down, norm_final):
    B, S, D = x.shape
    E = w_router.shape[-1]
    cfg = Cfg(batch=B, seq=S, d_model=D, d_ff=w_gate.shape[-1], n_experts=E, cap=2 * S // E)
    return _forward(cfg, x, norm_mix, w_in, sink_a, q_norm_b, k_norm_b, w_out, norm_ffn,
                    w_router, w_gate, w_up, w_down, norm_final)
```

```python
import functools
from typing import NamedTuple

import numpy as np
import jax
import jax.numpy as jnp
from jax import lax
from jax.experimental import pallas as pl
from jax.experimental.pallas import tpu as pltpu

HEAD_DIM = 128
H_A, KV_A = 8, 2
H_B, KV_B = 8, 2
GROUP = 4
WINDOW = 128
BLOCK = 128
GRID_W = 64
ROPE_THETA = 10000.0
EPS = 1e-6
NEG_INF = -1e30
SCALE = HEAD_DIM ** -0.5
LOG2E = 1.4426950408889634
LANES = 128
SUBLANES = 8
N_PROJ_HEADS = H_A + 2 * KV_A + H_B + 2 * KV_B
MIB = 1024 * 1024

F32 = jnp.float32
BF16 = jnp.bfloat16
I32 = jnp.int32


class Cfg(NamedTuple):
    batch: int
    seq: int
    d_model: int
    d_ff: int
    n_experts: int
    cap: int


def _params(sem, vmem_mib):
    return pltpu.CompilerParams(dimension_semantics=sem, vmem_limit_bytes=vmem_mib * MIB)


def _resident(shape, index_map):
    return pl.BlockSpec(shape, index_map, pipeline_mode=pl.Buffered(1))


def _inproj_body(x_ref, g_ref, w_ref, qn_ref, kn_ref, cos_ref, sin_ref,
                 qa_ref, ka_ref, va_ref, qb_ref, kb_ref, vb_ref):
    xf = x_ref[...]
    ms = jnp.mean(xf * xf, axis=-1, keepdims=True)
    h = (xf * lax.rsqrt(ms + EPS) * g_ref[...]).astype(BF16)
    n_a = (H_A + 2 * KV_A) * HEAD_DIM
    proj_b = jnp.dot(h, w_ref[:, n_a:], preferred_element_type=F32)
    proj_a = jnp.dot(h, w_ref[:, :n_a], preferred_element_type=F32)
    cos = cos_ref[...]
    sin = sin_ref[...]
    lane = lax.broadcasted_iota(I32, (1, LANES), 1)
    first = (lane % 64) < 32

    def head(c):
        if c * HEAD_DIM < n_a:
            return proj_a[:, c * HEAD_DIM:(c + 1) * HEAD_DIM]
        return proj_b[:, c * HEAD_DIM - n_a:(c + 1) * HEAD_DIM - n_a]

    def norm_rope(t, gain):
        m = jnp.mean(t * t, axis=-1, keepdims=True)
        tn = t * lax.rsqrt(m + EPS) * gain
        partner = jnp.where(first, pltpu.roll(tn, 96, 1), pltpu.roll(tn, 32, 1))
        return tn * cos + partner * sin

    c = H_A + 2 * KV_A
    for hh in range(H_B):
        qb_ref[hh] = (norm_rope(head(c + hh), qn_ref[...]) * (SCALE * LOG2E)).astype(BF16)
    c += H_B
    for hh in range(KV_B):
        kb_ref[hh] = norm_rope(head(c + hh), kn_ref[...]).astype(BF16)
    c += KV_B
    for hh in range(KV_B):
        vb_ref[hh] = head(c + hh).astype(BF16)
    c = 0
    for hh in range(H_A):
        qa_ref[hh] = (head(c + hh) * (SCALE * LOG2E)).astype(BF16)
    c += H_A
    for hh in range(KV_A):
        ka_ref[hh] = head(c + hh).astype(BF16)
    c += KV_A
    for hh in range(KV_A):
        va_ref[hh] = head(c + hh).astype(BF16)


def _inproj(cfg, x2, norm_mix, w_in_bf, q_norm, k_norm, cos_t, sin_t):
    T, D = x2.shape
    tm = 512
    sb = cfg.seq // tm
    heads = lambda n: jax.ShapeDtypeStruct((n, T, HEAD_DIM), BF16)
    hspec = lambda n: pl.BlockSpec((n, tm, HEAD_DIM), lambda i: (0, i, 0))
    return pl.pallas_call(
        _inproj_body,
        grid=(T // tm,),
        in_specs=[
            pl.BlockSpec((tm, D), lambda i: (i, 0)),
            _resident((1, D), lambda i: (0, 0)),
            _resident((D, N_PROJ_HEADS * HEAD_DIM), lambda i: (0, 0)),
            _resident((1, HEAD_DIM), lambda i: (0, 0)),
            _resident((1, HEAD_DIM), lambda i: (0, 0)),
            pl.BlockSpec((tm, HEAD_DIM), lambda i: (i % sb, 0)),
            pl.BlockSpec((tm, HEAD_DIM), lambda i: (i % sb, 0)),
        ],
        out_specs=[hspec(H_A), hspec(KV_A), hspec(KV_A), hspec(H_B), hspec(KV_B), hspec(KV_B)],
        out_shape=[heads(H_A), heads(KV_A), heads(KV_A), heads(H_B), heads(KV_B), heads(KV_B)],
        compiler_params=_params(("parallel",), 48),
        name="inproj",
    )(x2, norm_mix, w_in_bf, q_norm, k_norm, cos_t, sin_t)


def _win_body(sink_ref, slope_ref, q_ref, k_ref, v_ref, o_ref, kt_scr, vx_scr, bias_scr, sink_scr,
              *, seq, unroll):
    kvh = pl.program_id(1)
    span = 3 * BLOCK
    rows = GROUP * BLOCK
    nb = seq // BLOCK

    kt_scr[...] = k_ref[0].astype(F32).T.astype(BF16)
    vx_scr[:, :HEAD_DIM] = v_ref[0]
    vx_scr[:, HEAD_DIM:] = jnp.ones((seq, HEAD_DIM), BF16)
    qi = lax.broadcasted_iota(I32, (BLOCK, span), 0)
    kj = lax.broadcasted_iota(I32, (BLOCK, span), 1)
    for g in range(GROUP):
        hq = kvh * GROUP + g
        sink_scr[g * BLOCK:(g + 1) * BLOCK, :] = jnp.full((BLOCK, LANES), sink_ref[hq], F32) * LOG2E
        for place in range(3):
            dist = jnp.abs(qi + place * BLOCK - kj)
            bias = jnp.where(dist <= WINDOW, (slope_ref[hq] * dist.astype(F32)) * (-LOG2E), NEG_INF)
            bias_scr[place, g * BLOCK:(g + 1) * BLOCK, :] = bias

    def block(n):
        ws = pl.multiple_of(jnp.clip((n - 1) * BLOCK, 0, seq - span), BLOCK)
        q0 = pl.multiple_of(n * BLOCK, BLOCK)
        place = jnp.where(n == 0, 0, jnp.where(n == nb - 1, 2, 1))
        q = q_ref[:, pl.ds(q0, BLOCK), :].reshape(rows, HEAD_DIM)
        s = jnp.dot(q, kt_scr[:, pl.ds(ws, span)], preferred_element_type=F32) + bias_scr[place]
        sink = sink_scr[:, :1]
        m = jnp.maximum(jnp.max(s, axis=-1, keepdims=True), sink)
        p = jnp.exp2(s - m).astype(BF16)
        ox = jnp.dot(p, vx_scr[pl.ds(ws, span), :], preferred_element_type=F32)
        o = ox[:, :HEAD_DIM] / (ox[:, HEAD_DIM:] + jnp.exp2(sink - m))
        for g in range(GROUP):
            o_ref[pl.ds(q0, BLOCK), g * HEAD_DIM:(g + 1) * HEAD_DIM] = (
                o[g * BLOCK:(g + 1) * BLOCK].astype(BF16))

    def step(i, carry):
        for u in range(unroll):
            block(i * unroll + u)
        return carry

    lax.fori_loop(0, nb // unroll, step, 0)


def _win_attention(cfg, sink, slopes, qa, ka, va):
    S = cfg.seq
    T = cfg.batch * S
    assert S // BLOCK >= 4
    smem = pl.BlockSpec(memory_space=pltpu.SMEM)
    rows = GROUP * BLOCK
    return pl.pallas_call(
        functools.partial(_win_body, seq=S, unroll=4),
        grid=(cfg.batch, KV_A),
        in_specs=[
            smem, smem,
            pl.BlockSpec((GROUP, S, HEAD_DIM), lambda b, k: (k, b, 0)),
            pl.BlockSpec((1, S, HEAD_DIM), lambda b, k: (k, b, 0)),
            pl.BlockSpec((1, S, HEAD_DIM), lambda b, k: (k, b, 0)),
        ],
        out_specs=pl.BlockSpec((S, GROUP * HEAD_DIM), lambda b, k: (b, k)),
        out_shape=jax.ShapeDtypeStruct((T, H_A * HEAD_DIM), BF16),
        scratch_shapes=[pltpu.VMEM((HEAD_DIM, S), BF16), pltpu.VMEM((S, 2 * HEAD_DIM), BF16),
                        pltpu.VMEM((3, rows, 3 * BLOCK), F32), pltpu.VMEM((rows, LANES), F32)],
        compiler_params=_params(("parallel", "parallel"), 40),
        name="win_attention",
    )(sink, slopes, qa, ka, va)


def _grid_body(q_ref, k_ref, v_ref, o_ref, kt_scr, vx_scr, *, seq, tq, unroll):
    kt_scr[...] = k_ref[0].astype(F32).T.astype(BF16)
    vx_scr[:, :HEAD_DIM] = v_ref[0]
    vx_scr[:, HEAD_DIM:] = jnp.ones((seq, HEAD_DIM), BF16)

    def tile(g, r0):
        q = q_ref[g, pl.ds(r0, tq), :]
        s = jnp.dot(q, kt_scr[...], preferred_element_type=F32)
        m = jnp.max(s, axis=-1, keepdims=True)
        p = jnp.exp2(s - m).astype(BF16)
        ox = jnp.dot(p, vx_scr[...], preferred_element_type=F32)
        o = ox[:, :HEAD_DIM] / ox[:, HEAD_DIM:]
        o_ref[pl.ds(r0, tq), g * HEAD_DIM:(g + 1) * HEAD_DIM] = o.astype(BF16)

    def step(i, carry):
        for u in range(unroll):
            r0 = pl.multiple_of((i * unroll + u) * tq, tq)
            for g in range(GROUP):
                tile(g, r0)
        return carry

    lax.fori_loop(0, seq // (tq * unroll), step, 0)


def _grid_attention(cfg, qb, kb, vb):
    S = cfg.seq
    T = cfg.batch * S
    tq = 256
    return pl.pallas_call(
        functools.partial(_grid_body, seq=S, tq=tq, unroll=2),
        grid=(cfg.batch, KV_B),
        in_specs=[
            pl.BlockSpec((GROUP, S, HEAD_DIM), lambda b, k: (k, b, 0)),
            pl.BlockSpec((1, S, HEAD_DIM), lambda b, k: (k, b, 0)),
            pl.BlockSpec((1, S, HEAD_DIM), lambda b, k: (k, b, 0)),
        ],
        out_specs=pl.BlockSpec((S, GROUP * HEAD_DIM), lambda b, k: (b, k)),
        out_shape=jax.ShapeDtypeStruct((T, H_B * HEAD_DIM), BF16),
        scratch_shapes=[pltpu.VMEM((HEAD_DIM, S), BF16), pltpu.VMEM((S, 2 * HEAD_DIM), BF16)],
        compiler_params=_params(("parallel", "parallel"), 48),
        name="grid_attention",
    )(qb, kb, vb)


def _outproj_body(oa_ref, ob_ref, x_ref, wa_ref, wb_ref, g_ref, wr_ref,
                  x1_ref, h_ref, aff_ref, *, tm, mc, half_tiles, n_exp):
    for c in range(tm // mc):
        rows = slice(c * mc, (c + 1) * mc)
        mix = jnp.dot(oa_ref[rows, :], wa_ref[...], preferred_element_type=F32)
        mix = mix + jnp.dot(ob_ref[rows, :], wb_ref[...], preferred_element_type=F32)
        x1 = x_ref[rows, :] + mix
        x1_ref[rows, :] = x1
        ms = jnp.mean(x1 * x1, axis=-1, keepdims=True)
        h = x1 * lax.rsqrt(ms + EPS) * g_ref[...]
        h_hi = h.astype(BF16)
        h_lo = (h - h_hi.astype(F32)).astype(BF16)
        r = (jnp.dot(h_hi, wr_ref[...], preferred_element_type=F32)
             + jnp.dot(h_lo, wr_ref[...], preferred_element_type=F32))
        logits = r[:, :n_exp] + r[:, n_exp:]
        e = jnp.exp(logits - jnp.max(logits, axis=-1, keepdims=True))
        aff_ref[rows, :] = e / jnp.sum(e, axis=-1, keepdims=True)
        for half in range(2):
            for j in range(half_tiles):
                c0 = (half * half_tiles + j) * LANES
                h_ref[0, half, pl.ds(c * mc * half_tiles + j, mc, stride=half_tiles), :] = (
                    h[:, c0:c0 + LANES])


def _outproj(cfg, oa, ob, x2, w_a, w_b, norm_ffn, w_router_parts):
    T, D = x2.shape
    E = cfg.n_experts
    S = cfg.seq
    tm = 512
    sb = S // tm
    ht = D // (2 * LANES)
    return pl.pallas_call(
        functools.partial(_outproj_body, tm=tm, mc=256, half_tiles=ht, n_exp=E),
        grid=(T // tm,),
        in_specs=[
            pl.BlockSpec((tm, oa.shape[1]), lambda i: (i, 0)),
            pl.BlockSpec((tm, ob.shape[1]), lambda i: (i, 0)),
            pl.BlockSpec((tm, D), lambda i: (i, 0)),
            _resident(w_a.shape, lambda i: (0, 0)),
            _resident(w_b.shape, lambda i: (0, 0)),
            _resident((1, D), lambda i: (0, 0)),
            _resident((D, 2 * E), lambda i: (0, 0)),
        ],
        out_specs=[
            pl.BlockSpec((tm, D), lambda i: (i, 0)),
            pl.BlockSpec((1, 2, tm * ht, LANES), lambda i: (i // sb, 0, i % sb, 0)),
            pl.BlockSpec((tm, E), lambda i: (i, 0)),
        ],
        out_shape=[
            jax.ShapeDtypeStruct((T, D), F32),
            jax.ShapeDtypeStruct((cfg.batch, 2, S * ht, LANES), F32),
            jax.ShapeDtypeStruct((T, E), F32),
        ],
        compiler_params=_params(("parallel",), 40),
        name="outproj_router",
    )(oa, ob, x2, w_a, w_b, norm_ffn, w_router_parts)


def _topk_body(aff_ref, idx_ref, c_scr, *, seq, cap, n_exp):
    a = aff_ref[...]
    bits = lax.bitcast_convert_type(a, I32)

    def bisect(i, lo):
        cand = lo | jnp.left_shift(jnp.int32(1), 30 - i)
        cnt = jnp.sum((bits >= cand).astype(I32), axis=0, keepdims=True)
        return jnp.where(cnt >= cap, cand, lo)

    thr = lax.fori_loop(0, 31, bisect, jnp.zeros((1, n_exp), I32))
    gt = bits > thr
    eq = bits == thr
    need = (cap - jnp.sum(gt.astype(I32), axis=0, keepdims=True)).astype(F32)

    rc = 256
    r_i = lax.broadcasted_iota(I32, (rc, rc), 0)
    c_i = lax.broadcasted_iota(I32, (rc, rc), 1)
    lower = (c_i <= r_i).astype(BF16)

    def prefix(mask_f32):
        carry = jnp.zeros((1, n_exp), F32)
        for k in range(seq // rc):
            blk = mask_f32[k * rc:(k + 1) * rc, :]
            inc = jnp.dot(lower, blk.astype(BF16), preferred_element_type=F32) + carry
            c_scr[k * rc:(k + 1) * rc, :] = inc
            carry = inc[rc - 1:rc, :]
        return c_scr[...]

    eq_f = eq.astype(F32)
    tie_rank = prefix(eq_f) - eq_f
    sel = jnp.logical_or(gt, jnp.logical_and(eq, tie_rank < need))
    prefix(sel.astype(F32))

    r_row = lax.broadcasted_iota(I32, (1, cap), 1).astype(F32)
    rows = 512
    for e in range(n_exp):
        def count(k, acc, e=e):
            r0 = pl.multiple_of(k * rows, rows)
            col = c_scr[pl.ds(r0, rows), e:e + 1]
            hit = (col <= r_row).astype(F32)
            return acc + jnp.sum(hit.reshape(rows // SUBLANES, SUBLANES, cap), axis=0)

        acc = lax.fori_loop(0, seq // rows, count, jnp.zeros((SUBLANES, cap), F32))
        idx_ref[0, e:e + 1, :] = jnp.sum(acc, axis=0, keepdims=True).astype(I32)


def _topk(cfg, aff):
    S, E, C = cfg.seq, cfg.n_experts, cfg.cap
    return pl.pallas_call(
        functools.partial(_topk_body, seq=S, cap=C, n_exp=E),
        grid=(cfg.batch,),
        in_specs=[pl.BlockSpec((S, E), lambda b: (b, 0))],
        out_specs=pl.BlockSpec((1, E, C), lambda b: (b, 0, 0)),
        out_shape=jax.ShapeDtypeStruct((cfg.batch, E, C), I32),
        scratch_shapes=[pltpu.VMEM((S, E), F32)],
        compiler_params=_params(("parallel",), 40),
        name="topk",
    )(aff)


def _gather_body(idx_ref, h_ref, o_ref, tile_scr, *, cap, half_tiles, stride):
    for m in range(cap):
        t = idx_ref[0, 0, m]
        row = pl.multiple_of(t * half_tiles, half_tiles)
        tile_scr[pl.ds(m, half_tiles, stride=stride), :] = h_ref[0, 0, pl.ds(row, half_tiles), :]
    for j in range(half_tiles):
        o_ref[0, :, j * LANES:(j + 1) * LANES] = tile_scr[pl.ds(j * stride, cap), :].astype(BF16)


def _gather(cfg, idx3, h_tm):
    B, S, E, C, D = cfg.batch, cfg.seq, cfg.n_experts, cfg.cap, cfg.d_model
    ht = D // (2 * LANES)
    stride = C + SUBLANES
    return pl.pallas_call(
        functools.partial(_gather_body, cap=C, half_tiles=ht, stride=stride),
        grid=(B, 2, E),
        in_specs=[
            pl.BlockSpec((1, 1, C), lambda b, hf, e: (b * E + e, 0, 0), memory_space=pltpu.SMEM),
            _resident((1, 1, S * ht, LANES), lambda b, hf, e: (b, hf, 0, 0)),
        ],
        out_specs=pl.BlockSpec((1, C, D // 2), lambda b, hf, e: (e, b, hf)),
        out_shape=jax.ShapeDtypeStruct((E, B * C, D), BF16),
        scratch_shapes=[pltpu.VMEM((ht * stride, LANES), F32)],
        compiler_params=_params(("parallel", "parallel", "arbitrary"), 40),
        name="gather_tokens",
    )(idx3, h_tm)


def _ffn_body(x_ref, wg_ref, wu_ref, wd_ref, o_ref, h_scr, *, tm, mc, tf, tn, half_tiles, n_f, n_n):
    j = pl.program_id(2)

    @pl.when(j < n_f)
    def _():
        wg = wg_ref[0].astype(BF16)
        wu = wu_ref[0].astype(BF16)
        hid = []
        for c in range(tm // mc):
            x = x_ref[0, c * mc:(c + 1) * mc, :]
            a = jnp.dot(x, wg, preferred_element_type=F32)
            u = jnp.dot(x, wu, preferred_element_type=F32)
            hid.append((a * jax.nn.sigmoid(a) * u).astype(BF16))
        for f in range(n_f):
            @pl.when(j == f)
            def _(f=f):
                for c in range(tm // mc):
                    h_scr[c * mc:(c + 1) * mc, f * tf:(f + 1) * tf] = hid[c]

    for n in range(n_n):
        @pl.when(j == n_f + n)
        def _(n=n):
            y = jnp.dot(h_scr[...], wd_ref[0].astype(BF16), preferred_element_type=F32)
            for c in range(tn // LANES):
                half, row = divmod(n * (tn // LANES) + c, half_tiles)
                o_ref[0, half, pl.ds(row, tm, stride=half_tiles), :] = y[:, c * LANES:(c + 1) * LANES]


def _ffn(cfg, xg, w_gate, w_up, w_down):
    E, D, F = cfg.n_experts, cfg.d_model, cfg.d_ff
    M = cfg.batch * cfg.cap
    tm = min(1024, M)
    tf = min(256, F)
    tn = 256
    ht = D // (2 * LANES)
    n_f = F // tf
    n_n = D // tn
    return pl.pallas_call(
        functools.partial(_ffn_body, tm=tm, mc=min(256, tm), tf=tf, tn=tn, half_tiles=ht, n_f=n_f,
                          n_n=n_n),
        grid=(E, M // tm, n_f + n_n),
        in_specs=[
            _resident((1, tm, D), lambda e, m, j: (e, m, 0)),
            pl.BlockSpec((1, D, tf), lambda e, m, j: (e, 0, jnp.minimum(j, n_f - 1))),
            pl.BlockSpec((1, D, tf), lambda e, m, j: (e, 0, jnp.minimum(j, n_f - 1))),
            pl.BlockSpec((1, F, tn), lambda e, m, j: (e, 0, jnp.maximum(j - n_f, 0))),
        ],
        out_specs=pl.BlockSpec((1, 2, tm * ht, LANES), lambda e, m, j: (e, 0, m, 0)),
        out_shape=jax.ShapeDtypeStruct((E, 2, M * ht, LANES), F32),
        scratch_shapes=[pltpu.VMEM((tm, F), BF16)],
        compiler_params=_params(("parallel", "parallel", "arbitrary"), 52),
        name="expert_ffn",
    )(xg, w_gate, w_up, w_down)


def _combine_body(idx_ref, gate_ref, eo_ref, o_ref, *, cap, half_tiles, unroll):
    e = pl.program_id(2)

    @pl.when(e == 0)
    def _():
        o_ref[...] = jnp.zeros(o_ref.shape, F32)

    def group(g, carry):
        base = g * unroll
        dsts, vals = [], []
        for u in range(unroll):
            t = idx_ref[0, 0, base + u]
            dst = pl.multiple_of(t * half_tiles, half_tiles)
            src = pl.multiple_of((base + u) * half_tiles, half_tiles)
            dsts.append(dst)
            vals.append(o_ref[0, 0, pl.ds(dst, half_tiles), :]
                        + eo_ref[0, 0, pl.ds(src, half_tiles), :] * gate_ref[0, 0, t])
        for u in range(unroll):
            o_ref[0, 0, pl.ds(dsts[u], half_tiles), :] = vals[u]
        return carry

    lax.fori_loop(0, cap // unroll, group, 0)


def _combine(cfg, idx3, gates3, eo_tm):
    B, S, E, C, D = cfg.batch, cfg.seq, cfg.n_experts, cfg.cap, cfg.d_model
    ht = D // (2 * LANES)
    return pl.pallas_call(
        functools.partial(_combine_body, cap=C, half_tiles=ht, unroll=8),
        grid=(B, 2, E),
        in_specs=[
            pl.BlockSpec((1, 1, C), lambda b, h, e: (b * E + e, 0, 0), memory_space=pltpu.SMEM),
            pl.BlockSpec((1, 1, S), lambda b, h, e: (e * B + b, 0, 0), memory_space=pltpu.SMEM),
            pl.BlockSpec((1, 1, C * ht, LANES), lambda b, h, e: (e, h, b, 0)),
        ],
        out_specs=pl.BlockSpec((1, 1, S * ht, LANES), lambda b, h, e: (b, h, 0, 0)),
        out_shape=jax.ShapeDtypeStruct((B, 2, S * ht, LANES), F32),
        compiler_params=_params(("parallel", "parallel", "arbitrary"), 48),
        name="combine",
    )(idx3, gates3, eo_tm)


def _final_body(f_ref, x_ref, g_ref, o_ref, *, tm, half_tiles, d_model):
    ss = jnp.zeros((tm, 1), F32)
    for j in range(2 * half_tiles):
        cols = slice(j * LANES, (j + 1) * LANES)
        y = x_ref[:, cols] + f_ref[0, j // half_tiles, pl.ds(j % half_tiles, tm, stride=half_tiles), :]
        o_ref[:, cols] = y
        ss = ss + jnp.sum(y * y, axis=-1, keepdims=True)
    o_ref[...] = o_ref[...] * lax.rsqrt(ss / d_model + EPS) * g_ref[...]


def _final(cfg, ffn_tm, x1, norm_final):
    T, D = x1.shape
    ht = D // (2 * LANES)
    tm = 256
    sb = cfg.seq // tm
    return pl.pallas_call(
        functools.partial(_final_body, tm=tm, half_tiles=ht, d_model=D),
        grid=(T // tm,),
        in_specs=[
            pl.BlockSpec((1, 2, tm * ht, LANES), lambda i: (i // sb, 0, i % sb, 0)),
            pl.BlockSpec((tm, D), lambda i: (i, 0)),
            _resident((1, D), lambda i: (0, 0)),
        ],
        out_specs=pl.BlockSpec((tm, D), lambda i: (i, 0)),
        out_shape=jax.ShapeDtypeStruct((T, D), F32),
        compiler_params=_params(("parallel",), 40),
        name="final_norm",
    )(ffn_tm, x1, norm_final)


def _rope_tables(seq):
    rows = seq // GRID_W
    row = jnp.broadcast_to(jnp.arange(rows)[:, None], (rows, GRID_W)).reshape(seq)
    col = jnp.broadcast_to(jnp.arange(GRID_W)[None, :], (rows, GRID_W)).reshape(seq)
    half = HEAD_DIM // 2
    inv_freq = ROPE_THETA ** (-jnp.arange(0, half, 2, dtype=F32) / half)
    ang_r = row.astype(F32)[:, None] * inv_freq[None, :]
    ang_c = col.astype(F32)[:, None] * inv_freq[None, :]
    cos_t = jnp.concatenate([jnp.cos(ang_r), jnp.cos(ang_r), jnp.cos(ang_c), jnp.cos(ang_c)], axis=-1)
    sin_t = jnp.concatenate([-jnp.sin(ang_r), jnp.sin(ang_r), -jnp.sin(ang_c), jnp.sin(ang_c)], axis=-1)
    return cos_t, sin_t


def _layer(cfg, x2, cos_t, sin_t, slopes, norm_mix, w_in, sink_a, q_norm_b, k_norm_b, w_out,
           norm_ffn, w_router, w_gate, w_up, w_down):
    B, S, E, C = cfg.batch, cfg.seq, cfg.n_experts, cfg.cap
    qa, ka, va, qb, kb, vb = _inproj(cfg, x2, norm_mix[None], w_in.astype(BF16),
                                     q_norm_b[None], k_norm_b[None], cos_t, sin_t)
    oa = _win_attention(cfg, sink_a, slopes, qa, ka, va)
    ob = _grid_attention(cfg, qb, kb, vb)
    w_out_bf = w_out.astype(BF16)
    n_a = H_A * HEAD_DIM
    wr_hi = w_router.astype(BF16)
    wr_lo = (w_router - wr_hi.astype(F32)).astype(BF16)
    x1, h_tm, aff = _outproj(cfg, oa, ob, x2, w_out_bf[:n_a], w_out_bf[n_a:], norm_ffn[None],
                             jnp.concatenate([wr_hi, wr_lo], axis=-1))
    idx = _topk(cfg, aff)
    idx3 = idx.reshape(B * E, 1, C)
    gates3 = aff.T.reshape(E * B, 1, S)
    xg = _gather(cfg, idx3, h_tm)
    eo_tm = _ffn(cfg, xg, w_gate, w_up, w_down)
    ffn_tm = _combine(cfg, idx3, gates3, eo_tm)
    return x1, ffn_tm


def _forward(cfg, x, norm_mix, w_in, sink_a, q_norm_b, k_norm_b, w_out, norm_ffn,
             w_router, w_gate, w_up, w_down, norm_final):
    B, S, D = x.shape
    assert norm_mix.shape[0] == 1, "single-layer stack only"
    cos_t, sin_t = _rope_tables(S)
    slopes = jnp.asarray(2.0 ** (-8.0 * np.arange(1, H_A + 1) / H_A), dtype=F32)
    x2 = x.reshape(B * S, D)
    x1, ffn_tm = _layer(cfg, x2, cos_t, sin_t, slopes, norm_mix[0], w_in[0], sink_a[0],
                        q_norm_b[0], k_norm_b[0], w_out[0], norm_ffn[0], w_router[0],
                        w_gate[0], w_up[0], w_down[0])
    out = _final(cfg, ffn_tm, x1, norm_final[None])
    return out.reshape(B, S, D)


def kernel(x, norm_mix, w_in, sink_a, q_norm_b, k_norm_b, w_out, norm_ffn, w_router, w_gate, w_up,
           w_down, norm_final):
    B, S, D = x.shape
    E = w_router.shape[-1]
    cfg = Cfg(batch=B, seq=S, d_model=D, d_ff=w_gate.shape[-1], n_experts=E, cap=2 * S // E)
    return _forward(cfg, x, norm_mix, w_in, sink_a, q_norm_b, k_norm_b, w_out, norm_ffn,
                    w_router, w_gate, w_up, w_down, norm_final)
```

```python
import functools
from typing import NamedTuple

import numpy as np
import jax
import jax.numpy as jnp
from jax import lax
from jax.experimental import pallas as pl
from jax.experimental.pallas import tpu as pltpu

HEAD_DIM = 128
H_A, KV_A = 8, 2
H_B, KV_B = 8, 2
GROUP = 4
WINDOW = 128
BLOCK = 128
GRID_W = 64
ROPE_THETA = 10000.0
EPS = 1e-6
NEG_INF = -1e30
SCALE = HEAD_DIM ** -0.5
LOG2E = 1.4426950408889634
LANES = 128
SUBLANES = 8
N_PROJ_HEADS = H_A + 2 * KV_A + H_B + 2 * KV_B
MIB = 1024 * 1024

F32 = jnp.float32
BF16 = jnp.bfloat16
I32 = jnp.int32


class Cfg(NamedTuple):
    batch: int
    seq: int
    d_model: int
    d_ff: int
    n_experts: int
    cap: int


def _params(sem, vmem_mib):
    return pltpu.CompilerParams(dimension_semantics=sem, vmem_limit_bytes=vmem_mib * MIB)


def _resident(shape, index_map):
    return pl.BlockSpec(shape, index_map, pipeline_mode=pl.Buffered(1))


def _inproj_body(x_ref, g_ref, w_ref, qn_ref, kn_ref, cos_ref, sin_ref,
                 qa_ref, ka_ref, va_ref, qb_ref, kb_ref, vb_ref):
    xf = x_ref[...]
    ms = jnp.mean(xf * xf, axis=-1, keepdims=True)
    h = (xf * lax.rsqrt(ms + EPS) * g_ref[...]).astype(BF16)
    n_a = (H_A + 2 * KV_A) * HEAD_DIM
    proj_b = jnp.dot(h, w_ref[:, n_a:], preferred_element_type=F32)
    proj_a = jnp.dot(h, w_ref[:, :n_a], preferred_element_type=F32)
    cos = cos_ref[...]
    sin = sin_ref[...]
    lane = lax.broadcasted_iota(I32, (1, LANES), 1)
    first = (lane % 64) < 32

    def head(c):
        if c * HEAD_DIM < n_a:
            return proj_a[:, c * HEAD_DIM:(c + 1) * HEAD_DIM]
        return proj_b[:, c * HEAD_DIM - n_a:(c + 1) * HEAD_DIM - n_a]

    def norm_rope(t, gain):
        m = jnp.mean(t * t, axis=-1, keepdims=True)
        tn = t * lax.rsqrt(m + EPS) * gain
        partner = jnp.where(first, pltpu.roll(tn, 96, 1), pltpu.roll(tn, 32, 1))
        return tn * cos + partner * sin

    c = H_A + 2 * KV_A
    for hh in range(H_B):
        qb_ref[hh] = (norm_rope(head(c + hh), qn_ref[...]) * (SCALE * LOG2E)).astype(BF16)
    c += H_B
    for hh in range(KV_B):
        kb_ref[hh] = norm_rope(head(c + hh), kn_ref[...]).astype(BF16)
    c += KV_B
    for hh in range(KV_B):
        vb_ref[hh] = head(c + hh).astype(BF16)
    c = 0
    for hh in range(H_A):
        qa_ref[hh] = (head(c + hh) * (SCALE * LOG2E)).astype(BF16)
    c += H_A
    for hh in range(KV_A):
        ka_ref[hh] = head(c + hh).astype(BF16)
    c += KV_A
    for hh in range(KV_A):
        va_ref[hh] = head(c + hh).astype(BF16)


def _inproj(cfg, x2, norm_mix, w_in_bf, q_norm, k_norm, cos_t, sin_t):
    T, D = x2.shape
    tm = 512
    sb = cfg.seq // tm
    heads = lambda n: jax.ShapeDtypeStruct((n, T, HEAD_DIM), BF16)
    hspec = lambda n: pl.BlockSpec((n, tm, HEAD_DIM), lambda i: (0, i, 0))
    return pl.pallas_call(
        _inproj_body,
        grid=(T // tm,),
        in_specs=[
            pl.BlockSpec((tm, D), lambda i: (i, 0)),
            _resident((1, D), lambda i: (0, 0)),
            _resident((D, N_PROJ_HEADS * HEAD_DIM), lambda i: (0, 0)),
            _resident((1, HEAD_DIM), lambda i: (0, 0)),
            _resident((1, HEAD_DIM), lambda i: (0, 0)),
            pl.BlockSpec((tm, HEAD_DIM), lambda i: (i % sb, 0)),
            pl.BlockSpec((tm, HEAD_DIM), lambda i: (i % sb, 0)),
        ],
        out_specs=[hspec(H_A), hspec(KV_A), hspec(KV_A), hspec(H_B), hspec(KV_B), hspec(KV_B)],
        out_shape=[heads(H_A), heads(KV_A), heads(KV_A), heads(H_B), heads(KV_B), heads(KV_B)],
        compiler_params=_params(("parallel",), 48),
        name="inproj",
    )(x2, norm_mix, w_in_bf, q_norm, k_norm, cos_t, sin_t)


def _win_body(sink_ref, slope_ref, q_ref, k_ref, v_ref, o_ref, kt_scr, vx_scr, bias_scr, sink_scr,
              *, seq, unroll):
    kvh = pl.program_id(1)
    span = 3 * BLOCK
    rows = GROUP * BLOCK
    nb = seq // BLOCK

    kt_scr[...] = k_ref[0].astype(F32).T.astype(BF16)
    vx_scr[:, :HEAD_DIM] = v_ref[0]
    vx_scr[:, HEAD_DIM:] = jnp.ones((seq, HEAD_DIM), BF16)
    qi = lax.broadcasted_iota(I32, (BLOCK, span), 0)
    kj = lax.broadcasted_iota(I32, (BLOCK, span), 1)
    for g in range(GROUP):
        hq = kvh * GROUP + g
        sink_scr[g * BLOCK:(g + 1) * BLOCK, :] = jnp.full((BLOCK, LANES), sink_ref[hq], F32) * LOG2E
        for place in range(3):
            dist = jnp.abs(qi + place * BLOCK - kj)
            bias = jnp.where(dist <= WINDOW, (slope_ref[hq] * dist.astype(F32)) * (-LOG2E), NEG_INF)
            bias_scr[place, g * BLOCK:(g + 1) * BLOCK, :] = bias

    def block(n):
        ws = pl.multiple_of(jnp.clip((n - 1) * BLOCK, 0, seq - span), BLOCK)
        q0 = pl.multiple_of(n * BLOCK, BLOCK)
        place = jnp.where(n == 0, 0, jnp.where(n == nb - 1, 2, 1))
        q = q_ref[:, pl.ds(q0, BLOCK), :].reshape(rows, HEAD_DIM)
        s = jnp.dot(q, kt_scr[:, pl.ds(ws, span)], preferred_element_type=F32) + bias_scr[place]
        sink = sink_scr[:, :1]
        m = jnp.maximum(jnp.max(s, axis=-1, keepdims=True), sink)
        p = jnp.exp2(s - m).astype(BF16)
        ox = jnp.dot(p, vx_scr[pl.ds(ws, span), :], preferred_element_type=F32)
        o = ox[:, :HEAD_DIM] / (ox[:, HEAD_DIM:] + jnp.exp2(sink - m))
        for g in range(GROUP):
            o_ref[pl.ds(q0, BLOCK), g * HEAD_DIM:(g + 1) * HEAD_DIM] = (
                o[g * BLOCK:(g + 1) * BLOCK].astype(BF16))

    def step(i, carry):
        for u in range(unroll):
            block(i * unroll + u)
        return carry

    lax.fori_loop(0, nb // unroll, step, 0)


def _win_attention(cfg, sink, slopes, qa, ka, va):
    S = cfg.seq
    T = cfg.batch * S
    assert S // BLOCK >= 4
    smem = pl.BlockSpec(memory_space=pltpu.SMEM)
    rows = GROUP * BLOCK
    return pl.pallas_call(
        functools.partial(_win_body, seq=S, unroll=4),
        grid=(cfg.batch, KV_A),
        in_specs=[
            smem, smem,
            pl.BlockSpec((GROUP, S, HEAD_DIM), lambda b, k: (k, b, 0)),
            pl.BlockSpec((1, S, HEAD_DIM), lambda b, k: (k, b, 0)),
            pl.BlockSpec((1, S, HEAD_DIM), lambda b, k: (k, b, 0)),
        ],
        out_specs=pl.BlockSpec((S, GROUP * HEAD_DIM), lambda b, k: (b, k)),
        out_shape=jax.ShapeDtypeStruct((T, H_A * HEAD_DIM), BF16),
        scratch_shapes=[pltpu.VMEM((HEAD_DIM, S), BF16), pltpu.VMEM((S, 2 * HEAD_DIM), BF16),
                        pltpu.VMEM((3, rows, 3 * BLOCK), F32), pltpu.VMEM((rows, LANES), F32)],
        compiler_params=_params(("parallel", "parallel"), 40),
        name="win_attention",
    )(sink, slopes, qa, ka, va)


def _grid_body(q_ref, k_ref, v_ref, o_ref, kt_scr, vx_scr, *, seq, tq, unroll):
    kt_scr[...] = k_ref[0].astype(F32).T.astype(BF16)
    vx_scr[:, :HEAD_DIM] = v_ref[0]
    vx_scr[:, HEAD_DIM:] = jnp.ones((seq, HEAD_DIM), BF16)

    def tile(g, r0):
        q = q_ref[g, pl.ds(r0, tq), :]
        s = jnp.dot(q, kt_scr[...], preferred_element_type=F32)
        m = jnp.max(s, axis=-1, keepdims=True)
        p = jnp.exp2(s - m).astype(BF16)
        ox = jnp.dot(p, vx_scr[...], preferred_element_type=F32)
        o = ox[:, :HEAD_DIM] / ox[:, HEAD_DIM:]
        o_ref[pl.ds(r0, tq), g * HEAD_DIM:(g + 1) * HEAD_DIM] = o.astype(BF16)

    def step(i, carry):
        for u in range(unroll):
            r0 = pl.multiple_of((i * unroll + u) * tq, tq)
            for g in range(GROUP):
                tile(g, r0)
        return carry

    lax.fori_loop(0, seq // (tq * unroll), step, 0)


def _grid_attention(cfg, qb, kb, vb):
    S = cfg.seq
    T = cfg.batch * S
    tq = 256
    return pl.pallas_call(
        functools.partial(_grid_body, seq=S, tq=tq, unroll=2),
        grid=(cfg.batch, KV_B),
        in_specs=[
            pl.BlockSpec((GROUP, S, HEAD_DIM), lambda b, k: (k, b, 0)),
            pl.BlockSpec((1, S, HEAD_DIM), lambda b, k: (k, b, 0)),
            pl.BlockSpec((1, S, HEAD_DIM), lambda b, k: (k, b, 0)),
        ],
        out_specs=pl.BlockSpec((S, GROUP * HEAD_DIM), lambda b, k: (b, k)),
        out_shape=jax.ShapeDtypeStruct((T, H_B * HEAD_DIM), BF16),
        scratch_shapes=[pltpu.VMEM((HEAD_DIM, S), BF16), pltpu.VMEM((S, 2 * HEAD_DIM), BF16)],
        compiler_params=_params(("parallel", "parallel"), 48),
        name="grid_attention",
    )(qb, kb, vb)


def _outproj_body(oa_ref, ob_ref, x_ref, wa_ref, wb_ref, g_ref, wr_ref,
                  x1_ref, h_ref, aff_ref, *, tm, mc, half_tiles, n_exp):
    for c in range(tm // mc):
        rows = slice(c * mc, (c + 1) * mc)
        mix = jnp.dot(oa_ref[rows, :], wa_ref[...], preferred_element_type=F32)
        mix = mix + jnp.dot(ob_ref[rows, :], wb_ref[...], preferred_element_type=F32)
        x1 = x_ref[rows, :] + mix
        x1_ref[rows, :] = x1
        ms = jnp.mean(x1 * x1, axis=-1, keepdims=True)
        h = x1 * lax.rsqrt(ms + EPS) * g_ref[...]
        h_hi = h.astype(BF16)
        h_lo = (h - h_hi.astype(F32)).astype(BF16)
        r = (jnp.dot(h_hi, wr_ref[...], preferred_element_type=F32)
             + jnp.dot(h_lo, wr_ref[...], preferred_element_type=F32))
        logits = r[:, :n_exp] + r[:, n_exp:]
        e = jnp.exp(logits - jnp.max(logits, axis=-1, keepdims=True))
        aff_ref[rows, :] = e / jnp.sum(e, axis=-1, keepdims=True)
        for half in range(2):
            for j in range(half_tiles):
                c0 = (half * half_tiles + j) * LANES
                h_ref[0, half, pl.ds(c * mc * half_tiles + j, mc, stride=half_tiles), :] = (
                    h[:, c0:c0 + LANES])


def _outproj(cfg, oa, ob, x2, w_a, w_b, norm_ffn, w_router_parts):
    T, D = x2.shape
    E = cfg.n_experts
    S = cfg.seq
    tm = 512
    sb = S // tm
    ht = D // (2 * LANES)
    return pl.pallas_call(
        functools.partial(_outproj_body, tm=tm, mc=256, half_tiles=ht, n_exp=E),
        grid=(T // tm,),
        in_specs=[
            pl.BlockSpec((tm, oa.shape[1]), lambda i: (i, 0)),
            pl.BlockSpec((tm, ob.shape[1]), lambda i: (i, 0)),
            pl.BlockSpec((tm, D), lambda i: (i, 0)),
            _resident(w_a.shape, lambda i: (0, 0)),
            _resident(w_b.shape, lambda i: (0, 0)),
            _resident((1, D), lambda i: (0, 0)),
            _resident((D, 2 * E), lambda i: (0, 0)),
        ],
        out_specs=[
            pl.BlockSpec((tm, D), lambda i: (i, 0)),
            pl.BlockSpec((1, 2, tm * ht, LANES), lambda i: (i // sb, 0, i % sb, 0)),
            pl.BlockSpec((tm, E), lambda i: (i, 0)),
        ],
        out_shape=[
            jax.ShapeDtypeStruct((T, D), F32),
            jax.ShapeDtypeStruct((cfg.batch, 2, S * ht, LANES), F32),
            jax.ShapeDtypeStruct((T, E), F32),
        ],
        compiler_params=_params(("parallel",), 40),
        name="outproj_router",
    )(oa, ob, x2, w_a, w_b, norm_ffn, w_router_parts)


def _topk_body(aff_ref, idx_ref, c_scr, *, seq, cap, n_exp):
    a = aff_ref[...]
    bits = lax.bitcast_convert_type(a, I32)

    def bisect(i, lo):
        cand = lo | jnp.left_shift(jnp.int32(1), 30 - i)
        cnt = jnp.sum((bits >= cand).astype(I32), axis=0, keepdims=True)
        return jnp.where(cnt >= cap, cand, lo)

    thr = lax.fori_loop(0, 31, bisect, jnp.zeros((1, n_exp), I32))
    gt = bits > thr
    eq = bits == thr
    need = (cap - jnp.sum(gt.astype(I32), axis=0, keepdims=True)).astype(F32)

    rc = 256
    r_i = lax.broadcasted_iota(I32, (rc, rc), 0)
    c_i = lax.broadcasted_iota(I32, (rc, rc), 1)
    lower = (c_i <= r_i).astype(BF16)

    def prefix(mask_f32):
        carry = jnp.zeros((1, n_exp), F32)
        for k in range(seq // rc):
            blk = mask_f32[k * rc:(k + 1) * rc, :]
            inc = jnp.dot(lower, blk.astype(BF16), preferred_element_type=F32) + carry
            c_scr[k * rc:(k + 1) * rc, :] = inc
            carry = inc[rc - 1:rc, :]
        return c_scr[...]

    eq_f = eq.astype(F32)
    tie_rank = prefix(eq_f) - eq_f
    sel = jnp.logical_or(gt, jnp.logical_and(eq, tie_rank < need))
    prefix(sel.astype(F32))

    r_row = lax.broadcasted_iota(I32, (1, cap), 1).astype(F32)
    rows = 512
    for e in range(n_exp):
        def count(k, acc, e=e):
            r0 = pl.multiple_of(k * rows, rows)
            col = c_scr[pl.ds(r0, rows), e:e + 1]
            hit = (col <= r_row).astype(F32)
            return acc + jnp.sum(hit.reshape(rows // SUBLANES, SUBLANES, cap), axis=0)

        acc = lax.fori_loop(0, seq // rows, count, jnp.zeros((SUBLANES, cap), F32))
        idx_ref[0, e:e + 1, :] = jnp.sum(acc, axis=0, keepdims=True).astype(I32)


def _topk(cfg, aff):
    S, E, C = cfg.seq, cfg.n_experts, cfg.cap
    return pl.pallas_call(
        functools.partial(_topk_body, seq=S, cap=C, n_exp=E),
        grid=(cfg.batch,),
        in_specs=[pl.BlockSpec((S, E), lambda b: (b, 0))],
        out_specs=pl.BlockSpec((1, E, C), lambda b: (b, 0, 0)),
        out_shape=jax.ShapeDtypeStruct((cfg.batch, E, C), I32),
        scratch_shapes=[pltpu.VMEM((S, E), F32)],
        compiler_params=_params(("parallel",), 40),
        name="topk",
    )(aff)


def _gather_body(idx_ref, h_ref, o_ref, tile_scr, *, cap, half_tiles, stride):
    for m in range(cap):
        t = idx_ref[0, 0, m]
        row = pl.multiple_of(t * half_tiles, half_tiles)
        tile_scr[pl.ds(m, half_tiles, stride=stride), :] = h_ref[0, 0, pl.ds(row, half_tiles), :]
    for j in range(half_tiles):
        o_ref[0, :, j * LANES:(j + 1) * LANES] = tile_scr[pl.ds(j * stride, cap), :].astype(BF16)


def _gather(cfg, idx3, h_tm):
    B, S, E, C, D = cfg.batch, cfg.seq, cfg.n_experts, cfg.cap, cfg.d_model
    ht = D // (2 * LANES)
    stride = C + SUBLANES
    return pl.pallas_call(
        functools.partial(_gather_body, cap=C, half_tiles=ht, stride=stride),
        grid=(B, 2, E),
        in_specs=[
            pl.BlockSpec((1, 1, C), lambda b, hf, e: (b * E + e, 0, 0), memory_space=pltpu.SMEM),
            pl.BlockSpec((1, 1, S * ht, LANES), lambda b, hf, e: (b, hf, 0, 0)),
        ],
        out_specs=pl.BlockSpec((1, C, D // 2), lambda b, hf, e: (e, b, hf)),
        out_shape=jax.ShapeDtypeStruct((E, B * C, D), BF16),
        scratch_shapes=[pltpu.VMEM((ht * stride, LANES), F32)],
        compiler_params=_params(("parallel", "parallel", "arbitrary"), 40),
        name="gather_tokens",
    )(idx3, h_tm)


def _ffn_body(x_ref, wg_ref, wu_ref, wd_ref, o_ref, h_scr, *, tm, mc, tf, tn, half_tiles, n_f, n_n):
    fh = pl.program_id(1)
    j = pl.program_id(2)

    chunks = [slice(r, r + mc) for r in range(0, tm, mc)]

    @pl.when(jnp.logical_and(fh == 0, j == 0))
    def _():
        o_ref[...] = jnp.zeros(o_ref.shape, F32)

    @pl.when(j < n_f)
    def _():
        wg = wg_ref[0].astype(BF16)
        wu = wu_ref[0].astype(BF16)
        hid = []
        for rs in chunks:
            x = x_ref[0, rs, :]
            a = jnp.dot(x, wg, preferred_element_type=F32)
            u = jnp.dot(x, wu, preferred_element_type=F32)
            hid.append((a * jax.nn.sigmoid(a) * u).astype(BF16))
        for f in range(n_f):
            @pl.when(j == f)
            def _(f=f):
                for rs, hc in zip(chunks, hid):
                    h_scr[rs, f * tf:(f + 1) * tf] = hc

    for n in range(n_n):
        @pl.when(j == n_f + n)
        def _(n=n):
            wd = wd_ref[0].astype(BF16)
            for rs in chunks:
                y = jnp.dot(h_scr[rs, :], wd, preferred_element_type=F32)
                for c in range(tn // LANES):
                    half, row = divmod(n * (tn // LANES) + c, half_tiles)
                    rows = pl.ds(rs.start * half_tiles + row, mc, stride=half_tiles)
                    o_ref[0, half, rows, :] = o_ref[0, half, rows, :] + y[:, c * LANES:(c + 1) * LANES]


def _ffn(cfg, xg, w_gate, w_up, w_down):
    E, D, F = cfg.n_experts, cfg.d_model, cfg.d_ff
    tm = cfg.batch * cfg.cap
    fhalf = F // 2
    tf = min(256, fhalf)
    tn = 256
    ht = D // (2 * LANES)
    n_f = fhalf // tf
    n_n = D // tn
    return pl.pallas_call(
        functools.partial(_ffn_body, tm=tm, mc=min(512, tm), tf=tf, tn=tn, half_tiles=ht, n_f=n_f,
                          n_n=n_n),
        grid=(E, 2, n_f + n_n),
        in_specs=[
            _resident((1, tm, D), lambda e, h, j: (e, 0, 0)),
            pl.BlockSpec((1, D, tf), lambda e, h, j: (e, 0, h * n_f + jnp.minimum(j, n_f - 1))),
            pl.BlockSpec((1, D, tf), lambda e, h, j: (e, 0, h * n_f + jnp.minimum(j, n_f - 1))),
            pl.BlockSpec((1, fhalf, tn), lambda e, h, j: (e, h, jnp.maximum(j - n_f, 0))),
        ],
        out_specs=pl.BlockSpec((1, 2, tm * ht, LANES), lambda e, h, j: (e, 0, 0, 0),
                               pipeline_mode=pl.Buffered(1)),
        out_shape=jax.ShapeDtypeStruct((E, 2, tm * ht, LANES), F32),
        scratch_shapes=[pltpu.VMEM((tm, fhalf), BF16)],
        compiler_params=_params(("parallel", "arbitrary", "arbitrary"), 52),
        name="expert_ffn",
    )(xg, w_gate, w_up, w_down)


def _combine_body(idx_ref, gate_ref, eo_ref, o_ref, *, cap, half_tiles, unroll):
    e = pl.program_id(2)

    @pl.when(e == 0)
    def _():
        o_ref[...] = jnp.zeros(o_ref.shape, F32)

    def group(g, carry):
        base = g * unroll
        dsts, vals = [], []
        for u in range(unroll):
            t = idx_ref[0, 0, base + u]
            dst = pl.multiple_of(t * half_tiles, half_tiles)
            src = pl.multiple_of((base + u) * half_tiles, half_tiles)
            dsts.append(dst)
            vals.append(o_ref[0, 0, pl.ds(dst, half_tiles), :]
                        + eo_ref[0, 0, pl.ds(src, half_tiles), :] * gate_ref[0, 0, t])
        for u in range(unroll):
            o_ref[0, 0, pl.ds(dsts[u], half_tiles), :] = vals[u]
        return carry

    lax.fori_loop(0, cap // unroll, group, 0)


def _combine(cfg, idx3, gates3, eo_tm):
    B, S, E, C, D = cfg.batch, cfg.seq, cfg.n_experts, cfg.cap, cfg.d_model
    ht = D // (2 * LANES)
    return pl.pallas_call(
        functools.partial(_combine_body, cap=C, half_tiles=ht, unroll=8),
        grid=(B, 2, E),
        in_specs=[
            pl.BlockSpec((1, 1, C), lambda b, h, e: (b * E + e, 0, 0), memory_space=pltpu.SMEM),
            pl.BlockSpec((1, 1, S), lambda b, h, e: (e * B + b, 0, 0), memory_space=pltpu.SMEM),
            pl.BlockSpec((1, 1, C * ht, LANES), lambda b, h, e: (e, h, b, 0)),
        ],
        out_specs=pl.BlockSpec((1, 1, S * ht, LANES), lambda b, h, e: (b, h, 0, 0)),
        out_shape=jax.ShapeDtypeStruct((B, 2, S * ht, LANES), F32),
        compiler_params=_params(("parallel", "parallel", "arbitrary"), 48),
        name="combine",
    )(idx3, gates3, eo_tm)


def _final_body(f_ref, x_ref, g_ref, o_ref, *, tm, half_tiles, d_model):
    ss = jnp.zeros((tm, 1), F32)
    for j in range(2 * half_tiles):
        cols = slice(j * LANES, (j + 1) * LANES)
        y = x_ref[:, cols] + f_ref[0, j // half_tiles, pl.ds(j % half_tiles, tm, stride=half_tiles), :]
        o_ref[:, cols] = y
        ss = ss + jnp.sum(y * y, axis=-1, keepdims=True)
    o_ref[...] = o_ref[...] * lax.rsqrt(ss / d_model + EPS) * g_ref[...]


def _final(cfg, ffn_tm, x1, norm_final):
    T, D = x1.shape
    ht = D // (2 * LANES)
    tm = 256
    sb = cfg.seq // tm
    return pl.pallas_call(
        functools.partial(_final_body, tm=tm, half_tiles=ht, d_model=D),
        grid=(T // tm,),
        in_specs=[
            pl.BlockSpec((1, 2, tm * ht, LANES), lambda i: (i // sb, 0, i % sb, 0)),
            pl.BlockSpec((tm, D), lambda i: (i, 0)),
            _resident((1, D), lambda i: (0, 0)),
        ],
        out_specs=pl.BlockSpec((tm, D), lambda i: (i, 0)),
        out_shape=jax.ShapeDtypeStruct((T, D), F32),
        compiler_params=_params(("parallel",), 40),
        name="final_norm",
    )(ffn_tm, x1, norm_final)


def _rope_tables(seq):
    rows = seq // GRID_W
    row = jnp.broadcast_to(jnp.arange(rows)[:, None], (rows, GRID_W)).reshape(seq)
    col = jnp.broadcast_to(jnp.arange(GRID_W)[None, :], (rows, GRID_W)).reshape(seq)
    half = HEAD_DIM // 2
    inv_freq = ROPE_THETA ** (-jnp.arange(0, half, 2, dtype=F32) / half)
    ang_r = row.astype(F32)[:, None] * inv_freq[None, :]
    ang_c = col.astype(F32)[:, None] * inv_freq[None, :]
    cos_t = jnp.concatenate([jnp.cos(ang_r), jnp.cos(ang_r), jnp.cos(ang_c), jnp.cos(ang_c)], axis=-1)
    sin_t = jnp.concatenate([-jnp.sin(ang_r), jnp.sin(ang_r), -jnp.sin(ang_c), jnp.sin(ang_c)], axis=-1)
    return cos_t, sin_t


def _layer(cfg, x2, cos_t, sin_t, slopes, norm_mix, w_in, sink_a, q_norm_b, k_norm_b, w_out,
           norm_ffn, w_router, w_gate, w_up, w_down):
    B, S, E, C = cfg.batch, cfg.seq, cfg.n_experts, cfg.cap
    qa, ka, va, qb, kb, vb = _inproj(cfg, x2, norm_mix[None], w_in.astype(BF16),
                                     q_norm_b[None], k_norm_b[None], cos_t, sin_t)
    oa = _win_attention(cfg, sink_a, slopes, qa, ka, va)
    ob = _grid_attention(cfg, qb, kb, vb)
    w_out_bf = w_out.astype(BF16)
    n_a = H_A * HEAD_DIM
    wr_hi = w_router.astype(BF16)
    wr_lo = (w_router - wr_hi.astype(F32)).astype(BF16)
    x1, h_tm, aff = _outproj(cfg, oa, ob, x2, w_out_bf[:n_a], w_out_bf[n_a:], norm_ffn[None],
                             jnp.concatenate([wr_hi, wr_lo], axis=-1))
    idx = _topk(cfg, aff)
    idx3 = idx.reshape(B * E, 1, C)
    gates3 = aff.T.reshape(E * B, 1, S)
    xg = _gather(cfg, idx3, h_tm)
    eo_tm = _ffn(cfg, xg, w_gate, w_up, w_down)
    ffn_tm = _combine(cfg, idx3, gates3, eo_tm)
    return x1, ffn_tm


def _forward(cfg, x, norm_mix, w_in, sink_a, q_norm_b, k_norm_b, w_out, norm_ffn,
             w_router, w_gate, w_up, w_down, norm_final):
    B, S, D = x.shape
    assert norm_mix.shape[0] == 1, "single-layer stack only"
    cos_t, sin_t = _rope_tables(S)
    slopes = jnp.asarray(2.0 ** (-8.0 * np.arange(1, H_A + 1) / H_A), dtype=F32)
    x2 = x.reshape(B * S, D)
    x1, ffn_tm = _layer(cfg, x2, cos_t, sin_t, slopes, norm_mix[0], w_in[0], sink_a[0],
                        q_norm_b[0], k_norm_b[0], w_out[0], norm_ffn[0], w_router[0],
                        w_gate[0], w_up[0], w_down[0])
    out = _final(cfg, ffn_tm, x1, norm_final[None])
    return out.reshape(B, S, D)


def kernel(x, norm_mix, w_in, sink_a, q_norm_b, k_norm_b, w_out, norm_ffn, w_router, w_gate, w_up,
           w_down, norm_final):
    B, S, D = x.shape
    E = w_router.shape[-1]
    cfg = Cfg(batch=B, seq=S, d_model=D, d_ff=w_gate.shape[-1], n_experts=E, cap=2 * S // E)
    return _forward(cfg, x, norm_mix, w_in, sink_a, q_norm_b, k_norm_b, w_out, norm_ffn,
                    w_router, w_gate, w_up, w_down, norm_final)
```

```python
import functools
from typing import NamedTuple

import numpy as np
import jax
import jax.numpy as jnp
from jax import lax
from jax.experimental import pallas as pl
from jax.experimental.pallas import tpu as pltpu

HEAD_DIM = 128
H_A, KV_A = 8, 2
H_B, KV_B = 8, 2
GROUP = 4
WINDOW = 128
BLOCK = 128
GRID_W = 64
ROPE_THETA = 10000.0
EPS = 1e-6
NEG_INF = -1e30
SCALE = HEAD_DIM ** -0.5
LOG2E = 1.4426950408889634
LANES = 128
SUBLANES = 8
N_PROJ_HEADS = H_A + 2 * KV_A + H_B + 2 * KV_B
MIB = 1024 * 1024

F32 = jnp.float32
BF16 = jnp.bfloat16
I32 = jnp.int32


class Cfg(NamedTuple):
    batch: int
    seq: int
    d_model: int
    d_ff: int
    n_experts: int
    cap: int


def _params(sem, vmem_mib):
    return pltpu.CompilerParams(dimension_semantics=sem, vmem_limit_bytes=vmem_mib * MIB)


def _resident(shape, index_map):
    return pl.BlockSpec(shape, index_map, pipeline_mode=pl.Buffered(1))


def _inproj_body(x_ref, g_ref, w_ref, qn_ref, kn_ref, cos_ref, sin_ref,
                 qa_ref, ka_ref, va_ref, qb_ref, kb_ref, vb_ref):
    xf = x_ref[...]
    ms = jnp.mean(xf * xf, axis=-1, keepdims=True)
    h = (xf * lax.rsqrt(ms + EPS) * g_ref[...]).astype(BF16)
    n_a = (H_A + 2 * KV_A) * HEAD_DIM
    proj_b = jnp.dot(h, w_ref[:, n_a:], preferred_element_type=F32)
    proj_a = jnp.dot(h, w_ref[:, :n_a], preferred_element_type=F32)
    cos = cos_ref[...]
    sin = sin_ref[...]
    lane = lax.broadcasted_iota(I32, (1, LANES), 1)
    first = (lane % 64) < 32

    def head(c):
        if c * HEAD_DIM < n_a:
            return proj_a[:, c * HEAD_DIM:(c + 1) * HEAD_DIM]
        return proj_b[:, c * HEAD_DIM - n_a:(c + 1) * HEAD_DIM - n_a]

    def norm_rope(t, gain):
        m = jnp.mean(t * t, axis=-1, keepdims=True)
        tn = t * lax.rsqrt(m + EPS) * gain
        partner = jnp.where(first, pltpu.roll(tn, 96, 1), pltpu.roll(tn, 32, 1))
        return tn * cos + partner * sin

    c = H_A + 2 * KV_A
    for hh in range(H_B):
        qb_ref[hh] = (norm_rope(head(c + hh), qn_ref[...]) * (SCALE * LOG2E)).astype(BF16)
    c += H_B
    for hh in range(KV_B):
        kb_ref[hh] = norm_rope(head(c + hh), kn_ref[...]).astype(BF16)
    c += KV_B
    for hh in range(KV_B):
        vb_ref[hh] = head(c + hh).astype(BF16)
    c = 0
    for hh in range(H_A):
        qa_ref[hh] = (head(c + hh) * (SCALE * LOG2E)).astype(BF16)
    c += H_A
    for hh in range(KV_A):
        ka_ref[hh] = head(c + hh).astype(BF16)
    c += KV_A
    for hh in range(KV_A):
        va_ref[hh] = head(c + hh).astype(BF16)


def _inproj(cfg, x2, norm_mix, w_in_bf, q_norm, k_norm, cos_t, sin_t):
    T, D = x2.shape
    tm = 512
    sb = cfg.seq // tm
    heads = lambda n: jax.ShapeDtypeStruct((n, T, HEAD_DIM), BF16)
    hspec = lambda n: pl.BlockSpec((n, tm, HEAD_DIM), lambda i: (0, i, 0))
    return pl.pallas_call(
        _inproj_body,
        grid=(T // tm,),
        in_specs=[
            pl.BlockSpec((tm, D), lambda i: (i, 0)),
            _resident((1, D), lambda i: (0, 0)),
            _resident((D, N_PROJ_HEADS * HEAD_DIM), lambda i: (0, 0)),
            _resident((1, HEAD_DIM), lambda i: (0, 0)),
            _resident((1, HEAD_DIM), lambda i: (0, 0)),
            pl.BlockSpec((tm, HEAD_DIM), lambda i: (i % sb, 0)),
            pl.BlockSpec((tm, HEAD_DIM), lambda i: (i % sb, 0)),
        ],
        out_specs=[hspec(H_A), hspec(KV_A), hspec(KV_A), hspec(H_B), hspec(KV_B), hspec(KV_B)],
        out_shape=[heads(H_A), heads(KV_A), heads(KV_A), heads(H_B), heads(KV_B), heads(KV_B)],
        compiler_params=_params(("parallel",), 48),
        name="inproj",
    )(x2, norm_mix, w_in_bf, q_norm, k_norm, cos_t, sin_t)


def _win_body(sink_ref, slope_ref, q_ref, k_ref, v_ref, o_ref, kt_scr, vx_scr, bias_scr, sink_scr,
              *, seq, unroll):
    kvh = pl.program_id(1)
    span = 3 * BLOCK
    rows = GROUP * BLOCK
    nb = seq // BLOCK

    kt_scr[...] = k_ref[0].astype(F32).T.astype(BF16)
    vx_scr[:, :HEAD_DIM] = v_ref[0]
    vx_scr[:, HEAD_DIM:] = jnp.ones((seq, HEAD_DIM), BF16)
    qi = lax.broadcasted_iota(I32, (BLOCK, span), 0)
    kj = lax.broadcasted_iota(I32, (BLOCK, span), 1)
    for g in range(GROUP):
        hq = kvh * GROUP + g
        sink_scr[g * BLOCK:(g + 1) * BLOCK, :] = jnp.full((BLOCK, LANES), sink_ref[hq], F32) * LOG2E
        for place in range(3):
            dist = jnp.abs(qi + place * BLOCK - kj)
            bias = jnp.where(dist <= WINDOW, (slope_ref[hq] * dist.astype(F32)) * (-LOG2E), NEG_INF)
            bias_scr[place, g * BLOCK:(g + 1) * BLOCK, :] = bias

    def block(n):
        ws = pl.multiple_of(jnp.clip((n - 1) * BLOCK, 0, seq - span), BLOCK)
        q0 = pl.multiple_of(n * BLOCK, BLOCK)
        place = jnp.where(n == 0, 0, jnp.where(n == nb - 1, 2, 1))
        q = q_ref[:, pl.ds(q0, BLOCK), :].reshape(rows, HEAD_DIM)
        s = jnp.dot(q, kt_scr[:, pl.ds(ws, span)], preferred_element_type=F32) + bias_scr[place]
        sink = sink_scr[:, :1]
        m = jnp.maximum(jnp.max(s, axis=-1, keepdims=True), sink)
        p = jnp.exp2(s - m).astype(BF16)
        ox = jnp.dot(p, vx_scr[pl.ds(ws, span), :], preferred_element_type=F32)
        o = ox[:, :HEAD_DIM] / (ox[:, HEAD_DIM:] + jnp.exp2(sink - m))
        for g in range(GROUP):
            o_ref[pl.ds(q0, BLOCK), g * HEAD_DIM:(g + 1) * HEAD_DIM] = (
                o[g * BLOCK:(g + 1) * BLOCK].astype(BF16))

    def step(i, carry):
        for u in range(unroll):
            block(i * unroll + u)
        return carry

    lax.fori_loop(0, nb // unroll, step, 0)


def _win_attention(cfg, sink, slopes, qa, ka, va):
    S = cfg.seq
    T = cfg.batch * S
    assert S // BLOCK >= 4
    smem = pl.BlockSpec(memory_space=pltpu.SMEM)
    rows = GROUP * BLOCK
    return pl.pallas_call(
        functools.partial(_win_body, seq=S, unroll=4),
        grid=(cfg.batch, KV_A),
        in_specs=[
            smem, smem,
            pl.BlockSpec((GROUP, S, HEAD_DIM), lambda b, k: (k, b, 0)),
            pl.BlockSpec((1, S, HEAD_DIM), lambda b, k: (k, b, 0)),
            pl.BlockSpec((1, S, HEAD_DIM), lambda b, k: (k, b, 0)),
        ],
        out_specs=pl.BlockSpec((S, GROUP * HEAD_DIM), lambda b, k: (b, k)),
        out_shape=jax.ShapeDtypeStruct((T, H_A * HEAD_DIM), BF16),
        scratch_shapes=[pltpu.VMEM((HEAD_DIM, S), BF16), pltpu.VMEM((S, 2 * HEAD_DIM), BF16),
                        pltpu.VMEM((3, rows, 3 * BLOCK), F32), pltpu.VMEM((rows, LANES), F32)],
        compiler_params=_params(("parallel", "parallel"), 40),
        name="win_attention",
    )(sink, slopes, qa, ka, va)


def _grid_body(q_ref, k_ref, v_ref, o_ref, kt_scr, vx_scr, *, seq, tq, unroll):
    kt_scr[...] = k_ref[0].astype(F32).T.astype(BF16)
    vx_scr[:, :HEAD_DIM] = v_ref[0]
    vx_scr[:, HEAD_DIM:] = jnp.ones((seq, HEAD_DIM), BF16)

    def tile(g, r0):
        q = q_ref[g, pl.ds(r0, tq), :]
        s = jnp.dot(q, kt_scr[...], preferred_element_type=F32)
        m = jnp.max(s, axis=-1, keepdims=True)
        p = jnp.exp2(s - m).astype(BF16)
        ox = jnp.dot(p, vx_scr[...], preferred_element_type=F32)
        o = ox[:, :HEAD_DIM] / ox[:, HEAD_DIM:]
        o_ref[pl.ds(r0, tq), g * HEAD_DIM:(g + 1) * HEAD_DIM] = o.astype(BF16)

    def step(i, carry):
        for u in range(unroll):
            r0 = pl.multiple_of((i * unroll + u) * tq, tq)
            for g in range(GROUP):
                tile(g, r0)
        return carry

    lax.fori_loop(0, seq // (tq * unroll), step, 0)


def _grid_attention(cfg, qb, kb, vb):
    S = cfg.seq
    T = cfg.batch * S
    tq = 256
    return pl.pallas_call(
        functools.partial(_grid_body, seq=S, tq=tq, unroll=4),
        grid=(cfg.batch, KV_B),
        in_specs=[
            pl.BlockSpec((GROUP, S, HEAD_DIM), lambda b, k: (k, b, 0)),
            pl.BlockSpec((1, S, HEAD_DIM), lambda b, k: (k, b, 0)),
            pl.BlockSpec((1, S, HEAD_DIM), lambda b, k: (k, b, 0)),
        ],
        out_specs=pl.BlockSpec((S, GROUP * HEAD_DIM), lambda b, k: (b, k)),
        out_shape=jax.ShapeDtypeStruct((T, H_B * HEAD_DIM), BF16),
        scratch_shapes=[pltpu.VMEM((HEAD_DIM, S), BF16), pltpu.VMEM((S, 2 * HEAD_DIM), BF16)],
        compiler_params=_params(("parallel", "parallel"), 48),
        name="grid_attention",
    )(qb, kb, vb)


def _outproj_body(oa_ref, ob_ref, x_ref, wa_ref, wb_ref, g_ref, wr_ref,
                  x1_ref, h_ref, aff_ref, *, tm, mc, half_tiles, n_exp):
    for c in range(tm // mc):
        rows = slice(c * mc, (c + 1) * mc)
        mix = jnp.dot(oa_ref[rows, :], wa_ref[...], preferred_element_type=F32)
        mix = mix + jnp.dot(ob_ref[rows, :], wb_ref[...], preferred_element_type=F32)
        x1 = x_ref[rows, :] + mix
        x1_ref[rows, :] = x1
        ms = jnp.mean(x1 * x1, axis=-1, keepdims=True)
        h = x1 * lax.rsqrt(ms + EPS) * g_ref[...]
        h_hi = h.astype(BF16)
        h_lo = (h - h_hi.astype(F32)).astype(BF16)
        r = (jnp.dot(h_hi, wr_ref[...], preferred_element_type=F32)
             + jnp.dot(h_lo, wr_ref[...], preferred_element_type=F32))
        logits = r[:, :n_exp] + r[:, n_exp:]
        e = jnp.exp(logits - jnp.max(logits, axis=-1, keepdims=True))
        aff_ref[rows, :] = e / jnp.sum(e, axis=-1, keepdims=True)
        for half in range(2):
            for j in range(half_tiles):
                c0 = (half * half_tiles + j) * LANES
                h_ref[0, half, pl.ds(c * mc * half_tiles + j, mc, stride=half_tiles), :] = (
                    h[:, c0:c0 + LANES])


def _outproj(cfg, oa, ob, x2, w_a, w_b, norm_ffn, w_router_parts):
    T, D = x2.shape
    E = cfg.n_experts
    S = cfg.seq
    tm = 512
    sb = S // tm
    ht = D // (2 * LANES)
    return pl.pallas_call(
        functools.partial(_outproj_body, tm=tm, mc=256, half_tiles=ht, n_exp=E),
        grid=(T // tm,),
        in_specs=[
            pl.BlockSpec((tm, oa.shape[1]), lambda i: (i, 0)),
            pl.BlockSpec((tm, ob.shape[1]), lambda i: (i, 0)),
            pl.BlockSpec((tm, D), lambda i: (i, 0)),
            _resident(w_a.shape, lambda i: (0, 0)),
            _resident(w_b.shape, lambda i: (0, 0)),
            _resident((1, D), lambda i: (0, 0)),
            _resident((D, 2 * E), lambda i: (0, 0)),
        ],
        out_specs=[
            pl.BlockSpec((tm, D), lambda i: (i, 0)),
            pl.BlockSpec((1, 2, tm * ht, LANES), lambda i: (i // sb, 0, i % sb, 0)),
            pl.BlockSpec((tm, E), lambda i: (i, 0)),
        ],
        out_shape=[
            jax.ShapeDtypeStruct((T, D), F32),
            jax.ShapeDtypeStruct((cfg.batch, 2, S * ht, LANES), F32),
            jax.ShapeDtypeStruct((T, E), F32),
        ],
        compiler_params=_params(("parallel",), 40),
        name="outproj_router",
    )(oa, ob, x2, w_a, w_b, norm_ffn, w_router_parts)


def _topk_body(aff_ref, idx_ref, c_scr, *, seq, cap, n_exp):
    a = aff_ref[...]
    bits = lax.bitcast_convert_type(a, I32)

    def bisect(i, lo):
        cand = lo | jnp.left_shift(jnp.int32(1), 30 - i)
        cnt = jnp.sum((bits >= cand).astype(I32), axis=0, keepdims=True)
        return jnp.where(cnt >= cap, cand, lo)

    thr = lax.fori_loop(0, 31, bisect, jnp.zeros((1, n_exp), I32))
    gt = bits > thr
    eq = bits == thr
    need = (cap - jnp.sum(gt.astype(I32), axis=0, keepdims=True)).astype(F32)

    rc = 256
    r_i = lax.broadcasted_iota(I32, (rc, rc), 0)
    c_i = lax.broadcasted_iota(I32, (rc, rc), 1)
    lower = (c_i <= r_i).astype(BF16)

    def prefix(mask_f32):
        carry = jnp.zeros((1, n_exp), F32)
        for k in range(seq // rc):
            blk = mask_f32[k * rc:(k + 1) * rc, :]
            inc = jnp.dot(lower, blk.astype(BF16), preferred_element_type=F32) + carry
            c_scr[k * rc:(k + 1) * rc, :] = inc
            carry = inc[rc - 1:rc, :]
        return c_scr[...]

    eq_f = eq.astype(F32)
    tie_rank = prefix(eq_f) - eq_f
    sel = jnp.logical_or(gt, jnp.logical_and(eq, tie_rank < need))
    prefix(sel.astype(F32))

    r_row = lax.broadcasted_iota(I32, (1, cap), 1).astype(F32)
    rows = 512
    for e in range(n_exp):
        def count(k, acc, e=e):
            r0 = pl.multiple_of(k * rows, rows)
            col = c_scr[pl.ds(r0, rows), e:e + 1]
            hit = (col <= r_row).astype(F32)
            return acc + jnp.sum(hit.reshape(rows // SUBLANES, SUBLANES, cap), axis=0)

        acc = lax.fori_loop(0, seq // rows, count, jnp.zeros((SUBLANES, cap), F32))
        idx_ref[0, e:e + 1, :] = jnp.sum(acc, axis=0, keepdims=True).astype(I32)


def _topk(cfg, aff):
    S, E, C = cfg.seq, cfg.n_experts, cfg.cap
    return pl.pallas_call(
        functools.partial(_topk_body, seq=S, cap=C, n_exp=E),
        grid=(cfg.batch,),
        in_specs=[pl.BlockSpec((S, E), lambda b: (b, 0))],
        out_specs=pl.BlockSpec((1, E, C), lambda b: (b, 0, 0)),
        out_shape=jax.ShapeDtypeStruct((cfg.batch, E, C), I32),
        scratch_shapes=[pltpu.VMEM((S, E), F32)],
        compiler_params=_params(("parallel",), 40),
        name="topk",
    )(aff)


def _gather_body(idx_ref, h_ref, o_ref, tile_scr, *, cap, half_tiles, stride):
    for m in range(cap):
        t = idx_ref[0, 0, m]
        row = pl.multiple_of(t * half_tiles, half_tiles)
        tile_scr[pl.ds(m, half_tiles, stride=stride), :] = h_ref[0, 0, pl.ds(row, half_tiles), :]
    for j in range(half_tiles):
        o_ref[0, :, j * LANES:(j + 1) * LANES] = tile_scr[pl.ds(j * stride, cap), :].astype(BF16)


def _gather(cfg, idx3, h_tm):
    B, S, E, C, D = cfg.batch, cfg.seq, cfg.n_experts, cfg.cap, cfg.d_model
    ht = D // (2 * LANES)
    stride = C + SUBLANES
    return pl.pallas_call(
        functools.partial(_gather_body, cap=C, half_tiles=ht, stride=stride),
        grid=(B, 2, E),
        in_specs=[
            pl.BlockSpec((1, 1, C), lambda b, hf, e: (b * E + e, 0, 0), memory_space=pltpu.SMEM),
            pl.BlockSpec((1, 1, S * ht, LANES), lambda b, hf, e: (b, hf, 0, 0)),
        ],
        out_specs=pl.BlockSpec((1, C, D // 2), lambda b, hf, e: (e, b, hf)),
        out_shape=jax.ShapeDtypeStruct((E, B * C, D), BF16),
        scratch_shapes=[pltpu.VMEM((ht * stride, LANES), F32)],
        compiler_params=_params(("parallel", "parallel", "arbitrary"), 40),
        name="gather_tokens",
    )(idx3, h_tm)


def _ffn_body(x_ref, wg_ref, wu_ref, wd_ref, o_ref, h_scr, *, tm, mc, tf, tn, half_tiles, n_f, n_n):
    fh = pl.program_id(1)
    j = pl.program_id(2)

    chunks = [slice(r, r + mc) for r in range(0, tm, mc)]

    @pl.when(jnp.logical_and(fh == 0, j == 0))
    def _():
        o_ref[...] = jnp.zeros(o_ref.shape, F32)

    @pl.when(j < n_f)
    def _():
        wg = wg_ref[0].astype(BF16)
        wu = wu_ref[0].astype(BF16)
        hid = []
        for rs in chunks:
            x = x_ref[0, rs, :]
            a = jnp.dot(x, wg, preferred_element_type=F32)
            u = jnp.dot(x, wu, preferred_element_type=F32)
            hid.append((a * jax.nn.sigmoid(a) * u).astype(BF16))
        for f in range(n_f):
            @pl.when(j == f)
            def _(f=f):
                for rs, hc in zip(chunks, hid):
                    h_scr[rs, f * tf:(f + 1) * tf] = hc

    for n in range(n_n):
        @pl.when(j == n_f + n)
        def _(n=n):
            wd = wd_ref[0].astype(BF16)
            for rs in chunks:
                y = jnp.dot(h_scr[rs, :], wd, preferred_element_type=F32)
                for c in range(tn // LANES):
                    half, row = divmod(n * (tn // LANES) + c, half_tiles)
                    rows = pl.ds(rs.start * half_tiles + row, mc, stride=half_tiles)
                    o_ref[0, half, rows, :] = o_ref[0, half, rows, :] + y[:, c * LANES:(c + 1) * LANES]


def _ffn(cfg, xg, w_gate, w_up, w_down):
    E, D, F = cfg.n_experts, cfg.d_model, cfg.d_ff
    tm = cfg.batch * cfg.cap
    fhalf = F // 2
    tf = min(512, fhalf)
    tn = 256
    ht = D // (2 * LANES)
    n_f = fhalf // tf
    n_n = D // tn
    return pl.pallas_call(
        functools.partial(_ffn_body, tm=tm, mc=min(512, tm), tf=tf, tn=tn, half_tiles=ht, n_f=n_f,
                          n_n=n_n),
        grid=(E, 2, n_f + n_n),
        in_specs=[
            _resident((1, tm, D), lambda e, h, j: (e, 0, 0)),
            pl.BlockSpec((1, D, tf), lambda e, h, j: (e, 0, h * n_f + jnp.minimum(j, n_f - 1))),
            pl.BlockSpec((1, D, tf), lambda e, h, j: (e, 0, h * n_f + jnp.minimum(j, n_f - 1))),
            pl.BlockSpec((1, fhalf, tn), lambda e, h, j: (e, h, jnp.maximum(j - n_f, 0))),
        ],
        out_specs=pl.BlockSpec((1, 2, tm * ht, LANES), lambda e, h, j: (e, 0, 0, 0),
                               pipeline_mode=pl.Buffered(1)),
        out_shape=jax.ShapeDtypeStruct((E, 2, tm * ht, LANES), F32),
        scratch_shapes=[pltpu.VMEM((tm, fhalf), BF16)],
        compiler_params=_params(("parallel", "arbitrary", "arbitrary"), 62),
        name="expert_ffn",
    )(xg, w_gate, w_up, w_down)


def _combine_body(idx_ref, gate_ref, eo_ref, o_ref, *, cap, half_tiles, unroll):
    e = pl.program_id(2)

    @pl.when(e == 0)
    def _():
        o_ref[...] = jnp.zeros(o_ref.shape, F32)

    def group(g, carry):
        base = g * unroll
        dsts, vals = [], []
        for u in range(unroll):
            t = idx_ref[0, 0, base + u]
            dst = pl.multiple_of(t * half_tiles, half_tiles)
            src = pl.multiple_of((base + u) * half_tiles, half_tiles)
            dsts.append(dst)
            vals.append(o_ref[0, 0, pl.ds(dst, half_tiles), :]
                        + eo_ref[0, 0, pl.ds(src, half_tiles), :] * gate_ref[0, 0, t])
        for u in range(unroll):
            o_ref[0, 0, pl.ds(dsts[u], half_tiles), :] = vals[u]
        return carry

    lax.fori_loop(0, cap // unroll, group, 0)


def _combine(cfg, idx3, gates3, eo_tm):
    B, S, E, C, D = cfg.batch, cfg.seq, cfg.n_experts, cfg.cap, cfg.d_model
    ht = D // (2 * LANES)
    return pl.pallas_call(
        functools.partial(_combine_body, cap=C, half_tiles=ht, unroll=8),
        grid=(B, 2, E),
        in_specs=[
            pl.BlockSpec((1, 1, C), lambda b, h, e: (b * E + e, 0, 0), memory_space=pltpu.SMEM),
            pl.BlockSpec((1, 1, S), lambda b, h, e: (e * B + b, 0, 0), memory_space=pltpu.SMEM),
            pl.BlockSpec((1, 1, C * ht, LANES), lambda b, h, e: (e, h, b, 0)),
        ],
        out_specs=pl.BlockSpec((1, 1, S * ht, LANES), lambda b, h, e: (b, h, 0, 0)),
        out_shape=jax.ShapeDtypeStruct((B, 2, S * ht, LANES), F32),
        compiler_params=_params(("parallel", "parallel", "arbitrary"), 48),
        name="combine",
    )(idx3, gates3, eo_tm)


def _final_body(f_ref, x_ref, g_ref, o_ref, *, tm, half_tiles, d_model):
    ss = jnp.zeros((tm, 1), F32)
    for j in range(2 * half_tiles):
        cols = slice(j * LANES, (j + 1) * LANES)
        y = x_ref[:, cols] + f_ref[0, j // half_tiles, pl.ds(j % half_tiles, tm, stride=half_tiles), :]
        o_ref[:, cols] = y
        ss = ss + jnp.sum(y * y, axis=-1, keepdims=True)
    o_ref[...] = o_ref[...] * lax.rsqrt(ss / d_model + EPS) * g_ref[...]


def _final(cfg, ffn_tm, x1, norm_final):
    T, D = x1.shape
    ht = D // (2 * LANES)
    tm = 256
    sb = cfg.seq // tm
    return pl.pallas_call(
        functools.partial(_final_body, tm=tm, half_tiles=ht, d_model=D),
        grid=(T // tm,),
        in_specs=[
            pl.BlockSpec((1, 2, tm * ht, LANES), lambda i: (i // sb, 0, i % sb, 0)),
            pl.BlockSpec((tm, D), lambda i: (i, 0)),
            _resident((1, D), lambda i: (0, 0)),
        ],
        out_specs=pl.BlockSpec((tm, D), lambda i: (i, 0)),
        out_shape=jax.ShapeDtypeStruct((T, D), F32),
        compiler_params=_params(("parallel",), 40),
        name="final_norm",
    )(ffn_tm, x1, norm_final)


def _rope_tables(seq):
    rows = seq // GRID_W
    row = jnp.broadcast_to(jnp.arange(rows)[:, None], (rows, GRID_W)).reshape(seq)
    col = jnp.broadcast_to(jnp.arange(GRID_W)[None, :], (rows, GRID_W)).reshape(seq)
    half = HEAD_DIM // 2
    inv_freq = ROPE_THETA ** (-jnp.arange(0, half, 2, dtype=F32) / half)
    ang_r = row.astype(F32)[:, None] * inv_freq[None, :]
    ang_c = col.astype(F32)[:, None] * inv_freq[None, :]
    cos_t = jnp.concatenate([jnp.cos(ang_r), jnp.cos(ang_r), jnp.cos(ang_c), jnp.cos(ang_c)], axis=-1)
    sin_t = jnp.concatenate([-jnp.sin(ang_r), jnp.sin(ang_r), -jnp.sin(ang_c), jnp.sin(ang_c)], axis=-1)
    return cos_t, sin_t


def _layer(cfg, x2, cos_t, sin_t, slopes, norm_mix, w_in, sink_a, q_norm_b, k_norm_b, w_out,
           norm_ffn, w_router, w_gate, w_up, w_down):
    B, S, E, C = cfg.batch, cfg.seq, cfg.n_experts, cfg.cap
    qa, ka, va, qb, kb, vb = _inproj(cfg, x2, norm_mix[None], w_in.astype(BF16),
                                     q_norm_b[None], k_norm_b[None], cos_t, sin_t)
    oa = _win_attention(cfg, sink_a, slopes, qa, ka, va)
    ob = _grid_attention(cfg, qb, kb, vb)
    w_out_bf = w_out.astype(BF16)
    n_a = H_A * HEAD_DIM
    wr_hi = w_router.astype(BF16)
    wr_lo = (w_router - wr_hi.astype(F32)).astype(BF16)
    x1, h_tm, aff = _outproj(cfg, oa, ob, x2, w_out_bf[:n_a], w_out_bf[n_a:], norm_ffn[None],
                             jnp.concatenate([wr_hi, wr_lo], axis=-1))
    idx = _topk(cfg, aff)
    idx3 = idx.reshape(B * E, 1, C)
    gates3 = aff.T.reshape(E * B, 1, S)
    xg = _gather(cfg, idx3, h_tm)
    eo_tm = _ffn(cfg, xg, w_gate, w_up, w_down)
    ffn_tm = _combine(cfg, idx3, gates3, eo_tm)
    return x1, ffn_tm


def _forward(cfg, x, norm_mix, w_in, sink_a, q_norm_b, k_norm_b, w_out, norm_ffn,
             w_router, w_gate, w_up, w_down, norm_final):
    B, S, D = x.shape
    assert norm_mix.shape[0] == 1, "single-layer stack only"
    cos_t, sin_t = _rope_tables(S)
    slopes = jnp.asarray(2.0 ** (-8.0 * np.arange(1, H_A + 1) / H_A), dtype=F32)
    x2 = x.reshape(B * S, D)
    x1, ffn_tm = _layer(cfg, x2, cos_t, sin_t, slopes, norm_mix[0], w_in[0], sink_a[0],
                        q_norm_b[0], k_norm_b[0], w_out[0], norm_ffn[0], w_router[0],
                        w_gate[0], w_up[0], w_down[0])
    out = _final(cfg, ffn_tm, x1, norm_final[None])
    return out.reshape(B, S, D)


def kernel(x, norm_mix, w_in, sink_a, q_norm_b, k_norm_b, w_out, norm_ffn, w_router, w_gate, w_up,
           w_down, norm_final):
    B, S, D = x.shape
    E = w_router.shape[-1]
    cfg = Cfg(batch=B, seq=S, d_model=D, d_ff=w_gate.shape[-1], n_experts=E, cap=2 * S // E)
    return _forward(cfg, x, norm_mix, w_in, sink_a, q_norm_b, k_norm_b, w_out, norm_ffn,
                    w_router, w_gate, w_up, w_down, norm_final)
```

```python
import functools
from typing import NamedTuple

import numpy as np
import jax
import jax.numpy as jnp
from jax import lax
from jax.experimental import pallas as pl
from jax.experimental.pallas import tpu as pltpu

HEAD_DIM = 128
H_A, KV_A = 8, 2
H_B, KV_B = 8, 2
GROUP = 4
WINDOW = 128
BLOCK = 128
GRID_W = 64
ROPE_THETA = 10000.0
EPS = 1e-6
NEG_INF = -1e30
SCALE = HEAD_DIM ** -0.5
LOG2E = 1.4426950408889634
LANES = 128
SUBLANES = 8
N_PROJ_HEADS = H_A + 2 * KV_A + H_B + 2 * KV_B
MIB = 1024 * 1024
GATHER_EXPERTS_PER_STEP = 4
COMBINE_EXPERTS_PER_STEP = 2

F32 = jnp.float32
BF16 = jnp.bfloat16
I32 = jnp.int32


class Cfg(NamedTuple):
    batch: int
    seq: int
    d_model: int
    d_ff: int
    n_experts: int
    cap: int


def _params(sem, vmem_mib):
    return pltpu.CompilerParams(dimension_semantics=sem, vmem_limit_bytes=vmem_mib * MIB)


def _resident(shape, index_map):
    return pl.BlockSpec(shape, index_map, pipeline_mode=pl.Buffered(1))


def _inproj_body(x_ref, g_ref, w_ref, qn_ref, kn_ref, cos_ref, sin_ref,
                 qa_ref, ka_ref, va_ref, qb_ref, kb_ref, vb_ref):
    xf = x_ref[...]
    ms = jnp.mean(xf * xf, axis=-1, keepdims=True)
    h = (xf * lax.rsqrt(ms + EPS) * g_ref[...]).astype(BF16)
    n_a = (H_A + 2 * KV_A) * HEAD_DIM
    proj_b = jnp.dot(h, w_ref[:, n_a:], preferred_element_type=F32)
    proj_a = jnp.dot(h, w_ref[:, :n_a], preferred_element_type=F32)
    cos = cos_ref[...]
    sin = sin_ref[...]
    lane = lax.broadcasted_iota(I32, (1, LANES), 1)
    first = (lane % 64) < 32

    def head(c):
        if c * HEAD_DIM < n_a:
            return proj_a[:, c * HEAD_DIM:(c + 1) * HEAD_DIM]
        return proj_b[:, c * HEAD_DIM - n_a:(c + 1) * HEAD_DIM - n_a]

    def norm_rope(t, gain):
        m = jnp.mean(t * t, axis=-1, keepdims=True)
        tn = t * lax.rsqrt(m + EPS) * gain
        partner = jnp.where(first, pltpu.roll(tn, 96, 1), pltpu.roll(tn, 32, 1))
        return tn * cos + partner * sin

    c = H_A + 2 * KV_A
    for hh in range(H_B):
        qb_ref[hh] = (norm_rope(head(c + hh), qn_ref[...]) * (SCALE * LOG2E)).astype(BF16)
    c += H_B
    for hh in range(KV_B):
        kb_ref[hh] = norm_rope(head(c + hh), kn_ref[...]).astype(BF16)
    c += KV_B
    for hh in range(KV_B):
        vb_ref[hh] = head(c + hh).astype(BF16)
    c = 0
    for hh in range(H_A):
        qa_ref[hh] = (head(c + hh) * (SCALE * LOG2E)).astype(BF16)
    c += H_A
    for hh in range(KV_A):
        ka_ref[hh] = head(c + hh).astype(BF16)
    c += KV_A
    for hh in range(KV_A):
        va_ref[hh] = head(c + hh).astype(BF16)


def _inproj(cfg, x2, norm_mix, w_in_bf, q_norm, k_norm, cos_t, sin_t):
    T, D = x2.shape
    tm = 512
    sb = cfg.seq // tm
    heads = lambda n: jax.ShapeDtypeStruct((n, T, HEAD_DIM), BF16)
    hspec = lambda n: pl.BlockSpec((n, tm, HEAD_DIM), lambda i: (0, i, 0))
    return pl.pallas_call(
        _inproj_body,
        grid=(T // tm,),
        in_specs=[
            pl.BlockSpec((tm, D), lambda i: (i, 0)),
            _resident((1, D), lambda i: (0, 0)),
            _resident((D, N_PROJ_HEADS * HEAD_DIM), lambda i: (0, 0)),
            _resident((1, HEAD_DIM), lambda i: (0, 0)),
            _resident((1, HEAD_DIM), lambda i: (0, 0)),
            pl.BlockSpec((tm, HEAD_DIM), lambda i: (i % sb, 0)),
            pl.BlockSpec((tm, HEAD_DIM), lambda i: (i % sb, 0)),
        ],
        out_specs=[hspec(H_A), hspec(KV_A), hspec(KV_A), hspec(H_B), hspec(KV_B), hspec(KV_B)],
        out_shape=[heads(H_A), heads(KV_A), heads(KV_A), heads(H_B), heads(KV_B), heads(KV_B)],
        compiler_params=_params(("parallel",), 48),
        name="inproj",
    )(x2, norm_mix, w_in_bf, q_norm, k_norm, cos_t, sin_t)


def _win_body(sink_ref, slope_ref, q_ref, k_ref, v_ref, o_ref, kt_scr, vx_scr, bias_scr, sink_scr,
              *, seq, unroll):
    kvh = pl.program_id(1)
    span = 3 * BLOCK
    rows = GROUP * BLOCK
    nb = seq // BLOCK

    kt_scr[...] = k_ref[0].astype(F32).T.astype(BF16)
    vx_scr[:, :HEAD_DIM] = v_ref[0]
    vx_scr[:, HEAD_DIM:] = jnp.ones((seq, HEAD_DIM), BF16)
    qi = lax.broadcasted_iota(I32, (BLOCK, span), 0)
    kj = lax.broadcasted_iota(I32, (BLOCK, span), 1)
    for g in range(GROUP):
        hq = kvh * GROUP + g
        sink_scr[g * BLOCK:(g + 1) * BLOCK, :] = jnp.full((BLOCK, LANES), sink_ref[hq], F32) * LOG2E
        for place in range(3):
            dist = jnp.abs(qi + place * BLOCK - kj)
            bias = jnp.where(dist <= WINDOW, (slope_ref[hq] * dist.astype(F32)) * (-LOG2E), NEG_INF)
            bias_scr[place, g * BLOCK:(g + 1) * BLOCK, :] = bias

    def block(n):
        ws = pl.multiple_of(jnp.clip((n - 1) * BLOCK, 0, seq - span), BLOCK)
        q0 = pl.multiple_of(n * BLOCK, BLOCK)
        place = jnp.where(n == 0, 0, jnp.where(n == nb - 1, 2, 1))
        q = q_ref[:, pl.ds(q0, BLOCK), :].reshape(rows, HEAD_DIM)
        s = jnp.dot(q, kt_scr[:, pl.ds(ws, span)], preferred_element_type=F32) + bias_scr[place]
        sink = sink_scr[:, :1]
        m = jnp.maximum(jnp.max(s, axis=-1, keepdims=True), sink)
        p = jnp.exp2(s - m).astype(BF16)
        ox = jnp.dot(p, vx_scr[pl.ds(ws, span), :], preferred_element_type=F32)
        o = ox[:, :HEAD_DIM] / (ox[:, HEAD_DIM:] + jnp.exp2(sink - m))
        for g in range(GROUP):
            o_ref[pl.ds(q0, BLOCK), g * HEAD_DIM:(g + 1) * HEAD_DIM] = (
                o[g * BLOCK:(g + 1) * BLOCK].astype(BF16))

    def step(i, carry):
        for u in range(unroll):
            block(i * unroll + u)
        return carry

    lax.fori_loop(0, nb // unroll, step, 0)


def _win_attention(cfg, sink, slopes, qa, ka, va):
    S = cfg.seq
    T = cfg.batch * S
    assert S // BLOCK >= 4
    smem = pl.BlockSpec(memory_space=pltpu.SMEM)
    rows = GROUP * BLOCK
    return pl.pallas_call(
        functools.partial(_win_body, seq=S, unroll=8),
        grid=(cfg.batch, KV_A),
        in_specs=[
            smem, smem,
            pl.BlockSpec((GROUP, S, HEAD_DIM), lambda b, k: (k, b, 0)),
            pl.BlockSpec((1, S, HEAD_DIM), lambda b, k: (k, b, 0)),
            pl.BlockSpec((1, S, HEAD_DIM), lambda b, k: (k, b, 0)),
        ],
        out_specs=pl.BlockSpec((S, GROUP * HEAD_DIM), lambda b, k: (b, k)),
        out_shape=jax.ShapeDtypeStruct((T, H_A * HEAD_DIM), BF16),
        scratch_shapes=[pltpu.VMEM((HEAD_DIM, S), BF16), pltpu.VMEM((S, 2 * HEAD_DIM), BF16),
                        pltpu.VMEM((3, rows, 3 * BLOCK), F32), pltpu.VMEM((rows, LANES), F32)],
        compiler_params=_params(("parallel", "parallel"), 40),
        name="win_attention",
    )(sink, slopes, qa, ka, va)


def _grid_body(q_ref, k_ref, v_ref, o_ref, kt_scr, vx_scr, *, seq, tq, unroll):
    kt_scr[...] = k_ref[0].astype(F32).T.astype(BF16)
    vx_scr[:, :HEAD_DIM] = v_ref[0]
    vx_scr[:, HEAD_DIM:] = jnp.ones((seq, HEAD_DIM), BF16)

    def tile(g, r0):
        q = q_ref[g, pl.ds(r0, tq), :]
        s = jnp.dot(q, kt_scr[...], preferred_element_type=F32)
        m = jnp.max(s, axis=-1, keepdims=True)
        p = jnp.exp2(s - m).astype(BF16)
        ox = jnp.dot(p, vx_scr[...], preferred_element_type=F32)
        o = ox[:, :HEAD_DIM] / ox[:, HEAD_DIM:]
        o_ref[pl.ds(r0, tq), g * HEAD_DIM:(g + 1) * HEAD_DIM] = o.astype(BF16)

    def step(i, carry):
        for u in range(unroll):
            r0 = pl.multiple_of((i * unroll + u) * tq, tq)
            for g in range(GROUP):
                tile(g, r0)
        return carry

    lax.fori_loop(0, seq // (tq * unroll), step, 0)


def _grid_attention(cfg, qb, kb, vb):
    S = cfg.seq
    T = cfg.batch * S
    tq = 256
    return pl.pallas_call(
        functools.partial(_grid_body, seq=S, tq=tq, unroll=4),
        grid=(cfg.batch, KV_B),
        in_specs=[
            pl.BlockSpec((GROUP, S, HEAD_DIM), lambda b, k: (k, b, 0)),
            pl.BlockSpec((1, S, HEAD_DIM), lambda b, k: (k, b, 0)),
            pl.BlockSpec((1, S, HEAD_DIM), lambda b, k: (k, b, 0)),
        ],
        out_specs=pl.BlockSpec((S, GROUP * HEAD_DIM), lambda b, k: (b, k)),
        out_shape=jax.ShapeDtypeStruct((T, H_B * HEAD_DIM), BF16),
        scratch_shapes=[pltpu.VMEM((HEAD_DIM, S), BF16), pltpu.VMEM((S, 2 * HEAD_DIM), BF16)],
        compiler_params=_params(("parallel", "parallel"), 48),
        name="grid_attention",
    )(qb, kb, vb)


def _outproj_body(oa_ref, ob_ref, x_ref, wa_ref, wb_ref, g_ref, wr_ref,
                  x1_ref, h_ref, aff_ref, *, tm, mc, half_tiles, n_exp):
    for c in range(tm // mc):
        rows = slice(c * mc, (c + 1) * mc)
        mix = jnp.dot(oa_ref[rows, :], wa_ref[...], preferred_element_type=F32)
        mix = mix + jnp.dot(ob_ref[rows, :], wb_ref[...], preferred_element_type=F32)
        x1 = x_ref[rows, :] + mix
        x1_ref[rows, :] = x1
        ms = jnp.mean(x1 * x1, axis=-1, keepdims=True)
        h = x1 * lax.rsqrt(ms + EPS) * g_ref[...]
        h_hi = h.astype(BF16)
        h_lo = (h - h_hi.astype(F32)).astype(BF16)
        r = (jnp.dot(h_hi, wr_ref[...], preferred_element_type=F32)
             + jnp.dot(h_lo, wr_ref[...], preferred_element_type=F32))
        logits = r[:, :n_exp] + r[:, n_exp:]
        e = jnp.exp(logits - jnp.max(logits, axis=-1, keepdims=True))
        aff_ref[rows, :] = e / jnp.sum(e, axis=-1, keepdims=True)
        for half in range(2):
            for j in range(half_tiles):
                c0 = (half * half_tiles + j) * LANES
                h_ref[0, half, pl.ds(c * mc * half_tiles + j, mc, stride=half_tiles), :] = (
                    h[:, c0:c0 + LANES])


def _outproj(cfg, oa, ob, x2, w_a, w_b, norm_ffn, w_router_parts):
    T, D = x2.shape
    E = cfg.n_experts
    S = cfg.seq
    tm = 512
    sb = S // tm
    ht = D // (2 * LANES)
    return pl.pallas_call(
        functools.partial(_outproj_body, tm=tm, mc=256, half_tiles=ht, n_exp=E),
        grid=(T // tm,),
        in_specs=[
            pl.BlockSpec((tm, oa.shape[1]), lambda i: (i, 0)),
            pl.BlockSpec((tm, ob.shape[1]), lambda i: (i, 0)),
            pl.BlockSpec((tm, D), lambda i: (i, 0)),
            _resident(w_a.shape, lambda i: (0, 0)),
            _resident(w_b.shape, lambda i: (0, 0)),
            _resident((1, D), lambda i: (0, 0)),
            _resident((D, 2 * E), lambda i: (0, 0)),
        ],
        out_specs=[
            pl.BlockSpec((tm, D), lambda i: (i, 0)),
            pl.BlockSpec((1, 2, tm * ht, LANES), lambda i: (i // sb, 0, i % sb, 0)),
            pl.BlockSpec((tm, E), lambda i: (i, 0)),
        ],
        out_shape=[
            jax.ShapeDtypeStruct((T, D), F32),
            jax.ShapeDtypeStruct((cfg.batch, 2, S * ht, LANES), F32),
            jax.ShapeDtypeStruct((T, E), F32),
        ],
        compiler_params=_params(("parallel",), 40),
        name="outproj_router",
    )(oa, ob, x2, w_a, w_b, norm_ffn, w_router_parts)


def _topk_body(aff_ref, idx_ref, c_scr, *, seq, cap, n_exp):
    a = aff_ref[...]
    bits = lax.bitcast_convert_type(a, I32)

    def bisect(i, lo):
        cand = lo | jnp.left_shift(jnp.int32(1), 30 - i)
        cnt = jnp.sum((bits >= cand).astype(I32), axis=0, keepdims=True)
        return jnp.where(cnt >= cap, cand, lo)

    thr = lax.fori_loop(0, 31, bisect, jnp.zeros((1, n_exp), I32))
    gt = bits > thr
    eq = bits == thr
    need = (cap - jnp.sum(gt.astype(I32), axis=0, keepdims=True)).astype(F32)

    rc = 256
    r_i = lax.broadcasted_iota(I32, (rc, rc), 0)
    c_i = lax.broadcasted_iota(I32, (rc, rc), 1)
    lower = (c_i <= r_i).astype(BF16)

    def prefix(mask_f32):
        carry = jnp.zeros((1, n_exp), F32)
        for k in range(seq // rc):
            blk = mask_f32[k * rc:(k + 1) * rc, :]
            inc = jnp.dot(lower, blk.astype(BF16), preferred_element_type=F32) + carry
            c_scr[k * rc:(k + 1) * rc, :] = inc
            carry = inc[rc - 1:rc, :]
        return c_scr[...]

    eq_f = eq.astype(F32)
    tie_rank = prefix(eq_f) - eq_f
    sel = jnp.logical_or(gt, jnp.logical_and(eq, tie_rank < need))
    prefix(sel.astype(F32))

    r_row = lax.broadcasted_iota(I32, (1, cap), 1).astype(F32)
    rows = 512
    for e in range(n_exp):
        def count(k, acc, e=e):
            r0 = pl.multiple_of(k * rows, rows)
            col = c_scr[pl.ds(r0, rows), e:e + 1]
            hit = (col <= r_row).astype(F32)
            return acc + jnp.sum(hit.reshape(rows // SUBLANES, SUBLANES, cap), axis=0)

        acc = lax.fori_loop(0, seq // rows, count, jnp.zeros((SUBLANES, cap), F32))
        idx_ref[0, e:e + 1, :] = jnp.sum(acc, axis=0, keepdims=True).astype(I32)


def _topk(cfg, aff):
    S, E, C = cfg.seq, cfg.n_experts, cfg.cap
    return pl.pallas_call(
        functools.partial(_topk_body, seq=S, cap=C, n_exp=E),
        grid=(cfg.batch,),
        in_specs=[pl.BlockSpec((S, E), lambda b: (b, 0))],
        out_specs=pl.BlockSpec((1, E, C), lambda b: (b, 0, 0)),
        out_shape=jax.ShapeDtypeStruct((cfg.batch, E, C), I32),
        scratch_shapes=[pltpu.VMEM((S, E), F32)],
        compiler_params=_params(("parallel",), 40),
        name="topk",
    )(aff)


def _gather_body(idx_ref, h_ref, o_ref, tile_scr, *, cap, half_tiles, stride, group):
    def expert(x, carry):
        for m in range(cap):
            t = idx_ref[x, 0, m]
            row = pl.multiple_of(t * half_tiles, half_tiles)
            tile_scr[pl.ds(m, half_tiles, stride=stride), :] = h_ref[0, 0, pl.ds(row, half_tiles), :]
        for j in range(half_tiles):
            o_ref[x, :, j * LANES:(j + 1) * LANES] = tile_scr[pl.ds(j * stride, cap), :].astype(BF16)
        return carry

    lax.fori_loop(0, group, expert, 0)


def _gather(cfg, idx3, h_tm):
    B, S, E, C, D = cfg.batch, cfg.seq, cfg.n_experts, cfg.cap, cfg.d_model
    ht = D // (2 * LANES)
    stride = C + SUBLANES
    ge = GATHER_EXPERTS_PER_STEP
    return pl.pallas_call(
        functools.partial(_gather_body, cap=C, half_tiles=ht, stride=stride, group=ge),
        grid=(B, 2, E // ge),
        in_specs=[
            pl.BlockSpec((ge, 1, C), lambda b, hf, g: (b * (E // ge) + g, 0, 0), memory_space=pltpu.SMEM),
            pl.BlockSpec((1, 1, S * ht, LANES), lambda b, hf, g: (b, hf, 0, 0)),
        ],
        out_specs=pl.BlockSpec((ge, C, D // 2), lambda b, hf, g: (g, b, hf)),
        out_shape=jax.ShapeDtypeStruct((E, B * C, D), BF16),
        scratch_shapes=[pltpu.VMEM((ht * stride, LANES), F32)],
        compiler_params=_params(("parallel", "parallel", "arbitrary"), 48),
        name="gather_tokens",
    )(idx3, h_tm)


def _ffn_body(x_ref, wg_ref, wu_ref, wd_ref, o_ref, h_scr, *, tm, mc, tf, tn, half_tiles, n_f, n_n):
    fh = pl.program_id(1)
    j = pl.program_id(2)

    chunks = [slice(r, r + mc) for r in range(0, tm, mc)]

    @pl.when(jnp.logical_and(fh == 0, j == 0))
    def _():
        o_ref[...] = jnp.zeros(o_ref.shape, F32)

    @pl.when(j < n_f)
    def _():
        wg = wg_ref[0].astype(BF16)
        wu = wu_ref[0].astype(BF16)
        hid = []
        for rs in chunks:
            x = x_ref[0, rs, :]
            a = jnp.dot(x, wg, preferred_element_type=F32)
            u = jnp.dot(x, wu, preferred_element_type=F32)
            hid.append((a * jax.nn.sigmoid(a) * u).astype(BF16))
        for f in range(n_f):
            @pl.when(j == f)
            def _(f=f):
                for rs, hc in zip(chunks, hid):
                    h_scr[rs, f * tf:(f + 1) * tf] = hc

    for n in range(n_n):
        @pl.when(j == n_f + n)
        def _(n=n):
            wd = wd_ref[0].astype(BF16)
            for rs in chunks:
                y = jnp.dot(h_scr[rs, :], wd, preferred_element_type=F32)
                for c in range(tn // LANES):
                    half, row = divmod(n * (tn // LANES) + c, half_tiles)
                    rows = pl.ds(rs.start * half_tiles + row, mc, stride=half_tiles)
                    o_ref[0, half, rows, :] = o_ref[0, half, rows, :] + y[:, c * LANES:(c + 1) * LANES]


def _ffn(cfg, xg, w_gate, w_up, w_down):
    E, D, F = cfg.n_experts, cfg.d_model, cfg.d_ff
    tm = cfg.batch * cfg.cap
    fhalf = F // 2
    tf = min(512, fhalf)
    tn = 256
    ht = D // (2 * LANES)
    n_f = fhalf // tf
    n_n = D // tn
    return pl.pallas_call(
        functools.partial(_ffn_body, tm=tm, mc=min(512, tm), tf=tf, tn=tn, half_tiles=ht, n_f=n_f,
                          n_n=n_n),
        grid=(E, 2, n_f + n_n),
        in_specs=[
            _resident((1, tm, D), lambda e, h, j: (e, 0, 0)),
            pl.BlockSpec((1, D, tf), lambda e, h, j: (e, 0, h * n_f + jnp.minimum(j, n_f - 1))),
            pl.BlockSpec((1, D, tf), lambda e, h, j: (e, 0, h * n_f + jnp.minimum(j, n_f - 1))),
            pl.BlockSpec((1, fhalf, tn), lambda e, h, j: (e, h, jnp.maximum(j - n_f, 0))),
        ],
        out_specs=pl.BlockSpec((1, 2, tm * ht, LANES), lambda e, h, j: (e, 0, 0, 0),
                               pipeline_mode=pl.Buffered(1)),
        out_shape=jax.ShapeDtypeStruct((E, 2, tm * ht, LANES), F32),
        scratch_shapes=[pltpu.VMEM((tm, fhalf), BF16)],
        compiler_params=_params(("parallel", "arbitrary", "arbitrary"), 62),
        name="expert_ffn",
    )(xg, w_gate, w_up, w_down)


def _combine_body(idx_ref, gate_ref, eo_ref, o_ref, *, cap, half_tiles, unroll, group):
    @pl.when(pl.program_id(2) == 0)
    def _():
        o_ref[...] = jnp.zeros(o_ref.shape, F32)

    def expert(x, carry):
        def rows(g, c):
            base = g * unroll
            dsts, vals = [], []
            for u in range(unroll):
                t = idx_ref[x, 0, base + u]
                dst = pl.multiple_of(t * half_tiles, half_tiles)
                src = pl.multiple_of((base + u) * half_tiles, half_tiles)
                dsts.append(dst)
                vals.append(o_ref[0, 0, pl.ds(dst, half_tiles), :]
                            + eo_ref[x, 0, pl.ds(src, half_tiles), :] * gate_ref[x, 0, t])
            for u in range(unroll):
                o_ref[0, 0, pl.ds(dsts[u], half_tiles), :] = vals[u]
            return c

        return lax.fori_loop(0, cap // unroll, rows, carry)

    lax.fori_loop(0, group, expert, 0)


def _combine(cfg, idx3, gates3, eo_tm):
    B, S, E, C, D = cfg.batch, cfg.seq, cfg.n_experts, cfg.cap, cfg.d_model
    ht = D // (2 * LANES)
    ge = COMBINE_EXPERTS_PER_STEP
    return pl.pallas_call(
        functools.partial(_combine_body, cap=C, half_tiles=ht, unroll=8, group=ge),
        grid=(B, 2, E // ge),
        in_specs=[
            pl.BlockSpec((ge, 1, C), lambda b, h, g: (b * (E // ge) + g, 0, 0), memory_space=pltpu.SMEM),
            pl.BlockSpec((ge, 1, S), lambda b, h, g: (b * (E // ge) + g, 0, 0), memory_space=pltpu.SMEM),
            pl.BlockSpec((ge, 1, C * ht, LANES), lambda b, h, g: (g, h, b, 0)),
        ],
        out_specs=pl.BlockSpec((1, 1, S * ht, LANES), lambda b, h, g: (b, h, 0, 0)),
        out_shape=jax.ShapeDtypeStruct((B, 2, S * ht, LANES), F32),
        compiler_params=_params(("parallel", "parallel", "arbitrary"), 48),
        name="combine",
    )(idx3, gates3, eo_tm)


def _final_body(f_ref, x_ref, g_ref, o_ref, *, tm, half_tiles, d_model):
    ss = jnp.zeros((tm, 1), F32)
    for j in range(2 * half_tiles):
        cols = slice(j * LANES, (j + 1) * LANES)
        y = x_ref[:, cols] + f_ref[0, j // half_tiles, pl.ds(j % half_tiles, tm, stride=half_tiles), :]
        o_ref[:, cols] = y
        ss = ss + jnp.sum(y * y, axis=-1, keepdims=True)
    o_ref[...] = o_ref[...] * lax.rsqrt(ss / d_model + EPS) * g_ref[...]


def _final(cfg, ffn_tm, x1, norm_final):
    T, D = x1.shape
    ht = D // (2 * LANES)
    tm = 256
    sb = cfg.seq // tm
    return pl.pallas_call(
        functools.partial(_final_body, tm=tm, half_tiles=ht, d_model=D),
        grid=(T // tm,),
        in_specs=[
            pl.BlockSpec((1, 2, tm * ht, LANES), lambda i: (i // sb, 0, i % sb, 0)),
            pl.BlockSpec((tm, D), lambda i: (i, 0)),
            _resident((1, D), lambda i: (0, 0)),
        ],
        out_specs=pl.BlockSpec((tm, D), lambda i: (i, 0)),
        out_shape=jax.ShapeDtypeStruct((T, D), F32),
        compiler_params=_params(("parallel",), 40),
        name="final_norm",
    )(ffn_tm, x1, norm_final)


def _rope_tables(seq):
    f32 = np.float32
    rows = seq // GRID_W
    row = np.repeat(np.arange(rows), GRID_W).astype(f32)
    col = np.tile(np.arange(GRID_W), rows).astype(f32)
    half = HEAD_DIM // 2
    inv_freq = np.power(f32(ROPE_THETA), -np.arange(0, half, 2, dtype=f32) / f32(half)).astype(f32)
    ang_r = row[:, None] * inv_freq[None, :]
    ang_c = col[:, None] * inv_freq[None, :]
    cos_t = np.concatenate([np.cos(ang_r), np.cos(ang_r), np.cos(ang_c), np.cos(ang_c)], axis=-1)
    sin_t = np.concatenate([-np.sin(ang_r), np.sin(ang_r), -np.sin(ang_c), np.sin(ang_c)], axis=-1)
    return jnp.asarray(cos_t, F32), jnp.asarray(sin_t, F32)


def _layer(cfg, x2, cos_t, sin_t, slopes, norm_mix, w_in, sink_a, q_norm_b, k_norm_b, w_out,
           norm_ffn, w_router, w_gate, w_up, w_down):
    B, S, E, C = cfg.batch, cfg.seq, cfg.n_experts, cfg.cap
    qa, ka, va, qb, kb, vb = _inproj(cfg, x2, norm_mix[None], w_in.astype(BF16),
                                     q_norm_b[None], k_norm_b[None], cos_t, sin_t)
    oa = _win_attention(cfg, sink_a, slopes, qa, ka, va)
    ob = _grid_attention(cfg, qb, kb, vb)
    w_out_bf = w_out.astype(BF16)
    n_a = H_A * HEAD_DIM
    wr_hi = w_router.astype(BF16)
    wr_lo = (w_router - wr_hi.astype(F32)).astype(BF16)
    x1, h_tm, aff = _outproj(cfg, oa, ob, x2, w_out_bf[:n_a], w_out_bf[n_a:], norm_ffn[None],
                             jnp.concatenate([wr_hi, wr_lo], axis=-1))
    idx = _topk(cfg, aff)
    idx3 = idx.reshape(B * E, 1, C)
    gates3 = aff.reshape(B, S, E).transpose(0, 2, 1).reshape(B * E, 1, S)
    xg = _gather(cfg, idx3, h_tm)
    eo_tm = _ffn(cfg, xg, w_gate, w_up, w_down)
    ffn_tm = _combine(cfg, idx3, gates3, eo_tm)
    return x1, ffn_tm


def _forward(cfg, x, norm_mix, w_in, sink_a, q_norm_b, k_norm_b, w_out, norm_ffn,
             w_router, w_gate, w_up, w_down, norm_final):
    B, S, D = x.shape
    assert norm_mix.shape[0] == 1, "single-layer stack only"
    cos_t, sin_t = _rope_tables(S)
    slopes = jnp.asarray(2.0 ** (-8.0 * np.arange(1, H_A + 1) / H_A), dtype=F32)
    x2 = x.reshape(B * S, D)
    x1, ffn_tm = _layer(cfg, x2, cos_t, sin_t, slopes, norm_mix[0], w_in[0], sink_a[0],
                        q_norm_b[0], k_norm_b[0], w_out[0], norm_ffn[0], w_router[0],
                        w_gate[0], w_up[0], w_down[0])
    out = _final(cfg, ffn_tm, x1, norm_final[None])
    return out.reshape(B, S, D)


def kernel(x, norm_mix, w_in, sink_a, q_norm_b, k_norm_b, w_out, norm_ffn, w_router, w_gate, w_up,
           w_down, norm_final):
    B, S, D = x.shape
    E = w_router.shape[-1]
    cfg = Cfg(batch=B, seq=S, d_model=D, d_ff=w_gate.shape[-1], n_experts=E, cap=2 * S // E)
    return _forward(cfg, x, norm_mix, w_in, sink_a, q_norm_b, k_norm_b, w_out, norm_ffn,
                    w_router, w_gate, w_up, w_down, norm_final)
```

```python
import functools
from typing import NamedTuple

import numpy as np
import jax
import jax.numpy as jnp
from jax import lax
from jax.experimental import pallas as pl
from jax.experimental.pallas import tpu as pltpu

HEAD_DIM = 128
H_A, KV_A = 8, 2
H_B, KV_B = 8, 2
GROUP = 4
WINDOW = 128
BLOCK = 128
GRID_W = 64
ROPE_THETA = 10000.0
EPS = 1e-6
NEG_INF = -1e30
SCALE = HEAD_DIM ** -0.5
LOG2E = 1.4426950408889634
LANES = 128
SUBLANES = 8
N_PROJ_HEADS = H_A + 2 * KV_A + H_B + 2 * KV_B
MIB = 1024 * 1024
GATHER_EXPERTS_PER_STEP = 4
COMBINE_EXPERTS_PER_STEP = 1

F32 = jnp.float32
BF16 = jnp.bfloat16
I32 = jnp.int32


class Cfg(NamedTuple):
    batch: int
    seq: int
    d_model: int
    d_ff: int
    n_experts: int
    cap: int


def _params(sem, vmem_mib):
    return pltpu.CompilerParams(dimension_semantics=sem, vmem_limit_bytes=vmem_mib * MIB)


def _resident(shape, index_map):
    return pl.BlockSpec(shape, index_map, pipeline_mode=pl.Buffered(1))


def _inproj_body(x_ref, g_ref, w_ref, qn_ref, kn_ref, cos_ref, sin_ref,
                 qa_ref, ka_ref, va_ref, qb_ref, kb_ref, vb_ref):
    xf = x_ref[...]
    ms = jnp.mean(xf * xf, axis=-1, keepdims=True)
    h = (xf * lax.rsqrt(ms + EPS) * g_ref[...]).astype(BF16)
    n_a = (H_A + 2 * KV_A) * HEAD_DIM
    proj_b = jnp.dot(h, w_ref[:, n_a:], preferred_element_type=F32)
    proj_a = jnp.dot(h, w_ref[:, :n_a], preferred_element_type=F32)
    cos = cos_ref[...]
    sin = sin_ref[...]
    lane = lax.broadcasted_iota(I32, (1, LANES), 1)
    first = (lane % 64) < 32

    def head(c):
        if c * HEAD_DIM < n_a:
            return proj_a[:, c * HEAD_DIM:(c + 1) * HEAD_DIM]
        return proj_b[:, c * HEAD_DIM - n_a:(c + 1) * HEAD_DIM - n_a]

    def norm_rope(t, gain):
        m = jnp.mean(t * t, axis=-1, keepdims=True)
        tn = t * lax.rsqrt(m + EPS) * gain
        partner = jnp.where(first, pltpu.roll(tn, 96, 1), pltpu.roll(tn, 32, 1))
        return tn * cos + partner * sin

    c = H_A + 2 * KV_A
    for hh in range(H_B):
        qb_ref[hh] = (norm_rope(head(c + hh), qn_ref[...]) * (SCALE * LOG2E)).astype(BF16)
    c += H_B
    for hh in range(KV_B):
        kb_ref[hh] = norm_rope(head(c + hh), kn_ref[...]).astype(BF16)
    c += KV_B
    for hh in range(KV_B):
        vb_ref[hh] = head(c + hh).astype(BF16)
    c = 0
    for hh in range(H_A):
        qa_ref[hh] = (head(c + hh) * (SCALE * LOG2E)).astype(BF16)
    c += H_A
    for hh in range(KV_A):
        ka_ref[hh] = head(c + hh).astype(BF16)
    c += KV_A
    for hh in range(KV_A):
        va_ref[hh] = head(c + hh).astype(BF16)


def _inproj(cfg, x2, norm_mix, w_in_bf, q_norm, k_norm, cos_t, sin_t):
    T, D = x2.shape
    tm = 512
    sb = cfg.seq // tm
    heads = lambda n: jax.ShapeDtypeStruct((n, T, HEAD_DIM), BF16)
    hspec = lambda n: pl.BlockSpec((n, tm, HEAD_DIM), lambda i: (0, i, 0))
    return pl.pallas_call(
        _inproj_body,
        grid=(T // tm,),
        in_specs=[
            pl.BlockSpec((tm, D), lambda i: (i, 0)),
            _resident((1, D), lambda i: (0, 0)),
            _resident((D, N_PROJ_HEADS * HEAD_DIM), lambda i: (0, 0)),
            _resident((1, HEAD_DIM), lambda i: (0, 0)),
            _resident((1, HEAD_DIM), lambda i: (0, 0)),
            pl.BlockSpec((tm, HEAD_DIM), lambda i: (i % sb, 0)),
            pl.BlockSpec((tm, HEAD_DIM), lambda i: (i % sb, 0)),
        ],
        out_specs=[hspec(H_A), hspec(KV_A), hspec(KV_A), hspec(H_B), hspec(KV_B), hspec(KV_B)],
        out_shape=[heads(H_A), heads(KV_A), heads(KV_A), heads(H_B), heads(KV_B), heads(KV_B)],
        compiler_params=_params(("parallel",), 48),
        name="inproj",
    )(x2, norm_mix, w_in_bf, q_norm, k_norm, cos_t, sin_t)


def _win_body(sink_ref, slope_ref, q_ref, k_ref, v_ref, o_ref, kt_scr, vx_scr, bias_scr, sink_scr,
              *, seq, unroll):
    kvh = pl.program_id(1)
    span = 3 * BLOCK
    rows = GROUP * BLOCK
    nb = seq // BLOCK

    kt_scr[...] = k_ref[0].astype(F32).T.astype(BF16)
    vx_scr[:, :HEAD_DIM] = v_ref[0]
    vx_scr[:, HEAD_DIM:] = jnp.ones((seq, HEAD_DIM), BF16)
    qi = lax.broadcasted_iota(I32, (BLOCK, span), 0)
    kj = lax.broadcasted_iota(I32, (BLOCK, span), 1)
    for g in range(GROUP):
        hq = kvh * GROUP + g
        sink_scr[g * BLOCK:(g + 1) * BLOCK, :] = jnp.full((BLOCK, LANES), sink_ref[hq], F32) * LOG2E
        for place in range(3):
            dist = jnp.abs(qi + place * BLOCK - kj)
            bias = jnp.where(dist <= WINDOW, (slope_ref[hq] * dist.astype(F32)) * (-LOG2E), NEG_INF)
            bias_scr[place, g * BLOCK:(g + 1) * BLOCK, :] = bias

    def block(n):
        ws = pl.multiple_of(jnp.clip((n - 1) * BLOCK, 0, seq - span), BLOCK)
        q0 = pl.multiple_of(n * BLOCK, BLOCK)
        place = jnp.where(n == 0, 0, jnp.where(n == nb - 1, 2, 1))
        q = q_ref[:, pl.ds(q0, BLOCK), :].reshape(rows, HEAD_DIM)
        s = jnp.dot(q, kt_scr[:, pl.ds(ws, span)], preferred_element_type=F32) + bias_scr[place]
        sink = sink_scr[:, :1]
        m = jnp.maximum(jnp.max(s, axis=-1, keepdims=True), sink)
        p = jnp.exp2(s - m).astype(BF16)
        ox = jnp.dot(p, vx_scr[pl.ds(ws, span), :], preferred_element_type=F32)
        o = ox[:, :HEAD_DIM] / (ox[:, HEAD_DIM:] + jnp.exp2(sink - m))
        for g in range(GROUP):
            o_ref[pl.ds(q0, BLOCK), g * HEAD_DIM:(g + 1) * HEAD_DIM] = (
                o[g * BLOCK:(g + 1) * BLOCK].astype(BF16))

    def step(i, carry):
        for u in range(unroll):
            block(i * unroll + u)
        return carry

    lax.fori_loop(0, nb // unroll, step, 0)


def _win_attention(cfg, sink, slopes, qa, ka, va):
    S = cfg.seq
    T = cfg.batch * S
    assert S // BLOCK >= 4
    smem = pl.BlockSpec(memory_space=pltpu.SMEM)
    rows = GROUP * BLOCK
    return pl.pallas_call(
        functools.partial(_win_body, seq=S, unroll=8),
        grid=(cfg.batch, KV_A),
        in_specs=[
            smem, smem,
            pl.BlockSpec((GROUP, S, HEAD_DIM), lambda b, k: (k, b, 0)),
            pl.BlockSpec((1, S, HEAD_DIM), lambda b, k: (k, b, 0)),
            pl.BlockSpec((1, S, HEAD_DIM), lambda b, k: (k, b, 0)),
        ],
        out_specs=pl.BlockSpec((S, GROUP * HEAD_DIM), lambda b, k: (b, k)),
        out_shape=jax.ShapeDtypeStruct((T, H_A * HEAD_DIM), BF16),
        scratch_shapes=[pltpu.VMEM((HEAD_DIM, S), BF16), pltpu.VMEM((S, 2 * HEAD_DIM), BF16),
                        pltpu.VMEM((3, rows, 3 * BLOCK), F32), pltpu.VMEM((rows, LANES), F32)],
        compiler_params=_params(("parallel", "parallel"), 40),
        name="win_attention",
    )(sink, slopes, qa, ka, va)


def _grid_body(q_ref, k_ref, v_ref, o_ref, kt_scr, vx_scr, *, seq, tq, unroll):
    kt_scr[...] = k_ref[0].astype(F32).T.astype(BF16)
    vx_scr[:, :HEAD_DIM] = v_ref[0]
    vx_scr[:, HEAD_DIM:] = jnp.ones((seq, HEAD_DIM), BF16)

    def tile(g, r0):
        q = q_ref[g, pl.ds(r0, tq), :]
        s = jnp.dot(q, kt_scr[...], preferred_element_type=F32)
        m = jnp.max(s, axis=-1, keepdims=True)
        p = jnp.exp2(s - m).astype(BF16)
        ox = jnp.dot(p, vx_scr[...], preferred_element_type=F32)
        o = ox[:, :HEAD_DIM] / ox[:, HEAD_DIM:]
        o_ref[pl.ds(r0, tq), g * HEAD_DIM:(g + 1) * HEAD_DIM] = o.astype(BF16)

    def step(i, carry):
        for u in range(unroll):
            r0 = pl.multiple_of((i * unroll + u) * tq, tq)
            for g in range(GROUP):
                tile(g, r0)
        return carry

    lax.fori_loop(0, seq // (tq * unroll), step, 0)


def _grid_attention(cfg, qb, kb, vb):
    S = cfg.seq
    T = cfg.batch * S
    tq = 256
    return pl.pallas_call(
        functools.partial(_grid_body, seq=S, tq=tq, unroll=4),
        grid=(cfg.batch, KV_B),
        in_specs=[
            pl.BlockSpec((GROUP, S, HEAD_DIM), lambda b, k: (k, b, 0)),
            pl.BlockSpec((1, S, HEAD_DIM), lambda b, k: (k, b, 0)),
            pl.BlockSpec((1, S, HEAD_DIM), lambda b, k: (k, b, 0)),
        ],
        out_specs=pl.BlockSpec((S, GROUP * HEAD_DIM), lambda b, k: (b, k)),
        out_shape=jax.ShapeDtypeStruct((T, H_B * HEAD_DIM), BF16),
        scratch_shapes=[pltpu.VMEM((HEAD_DIM, S), BF16), pltpu.VMEM((S, 2 * HEAD_DIM), BF16)],
        compiler_params=_params(("parallel", "parallel"), 48),
        name="grid_attention",
    )(qb, kb, vb)


def _outproj_body(oa_ref, ob_ref, x_ref, wa_ref, wb_ref, g_ref, wr_ref,
                  x1_ref, h_ref, aff_ref, *, tm, mc, half_tiles, n_exp):
    for c in range(tm // mc):
        rows = slice(c * mc, (c + 1) * mc)
        mix = jnp.dot(oa_ref[rows, :], wa_ref[...], preferred_element_type=F32)
        mix = mix + jnp.dot(ob_ref[rows, :], wb_ref[...], preferred_element_type=F32)
        x1 = x_ref[rows, :] + mix
        x1_ref[rows, :] = x1
        ms = jnp.mean(x1 * x1, axis=-1, keepdims=True)
        h = x1 * lax.rsqrt(ms + EPS) * g_ref[...]
        h_hi = h.astype(BF16)
        h_lo = (h - h_hi.astype(F32)).astype(BF16)
        r = (jnp.dot(h_hi, wr_ref[...], preferred_element_type=F32)
             + jnp.dot(h_lo, wr_ref[...], preferred_element_type=F32))
        logits = r[:, :n_exp] + r[:, n_exp:]
        e = jnp.exp(logits - jnp.max(logits, axis=-1, keepdims=True))
        aff_ref[rows, :] = e / jnp.sum(e, axis=-1, keepdims=True)
        for half in range(2):
            for j in range(half_tiles):
                c0 = (half * half_tiles + j) * LANES
                h_ref[0, half, pl.ds(c * mc * half_tiles + j, mc, stride=half_tiles), :] = (
                    h[:, c0:c0 + LANES])


def _outproj(cfg, oa, ob, x2, w_a, w_b, norm_ffn, w_router_parts):
    T, D = x2.shape
    E = cfg.n_experts
    S = cfg.seq
    tm = 512
    sb = S // tm
    ht = D // (2 * LANES)
    return pl.pallas_call(
        functools.partial(_outproj_body, tm=tm, mc=256, half_tiles=ht, n_exp=E),
        grid=(T // tm,),
        in_specs=[
            pl.BlockSpec((tm, oa.shape[1]), lambda i: (i, 0)),
            pl.BlockSpec((tm, ob.shape[1]), lambda i: (i, 0)),
            pl.BlockSpec((tm, D), lambda i: (i, 0)),
            _resident(w_a.shape, lambda i: (0, 0)),
            _resident(w_b.shape, lambda i: (0, 0)),
            _resident((1, D), lambda i: (0, 0)),
            _resident((D, 2 * E), lambda i: (0, 0)),
        ],
        out_specs=[
            pl.BlockSpec((tm, D), lambda i: (i, 0)),
            pl.BlockSpec((1, 2, tm * ht, LANES), lambda i: (i // sb, 0, i % sb, 0)),
            pl.BlockSpec((tm, E), lambda i: (i, 0)),
        ],
        out_shape=[
            jax.ShapeDtypeStruct((T, D), F32),
            jax.ShapeDtypeStruct((cfg.batch, 2, S * ht, LANES), F32),
            jax.ShapeDtypeStruct((T, E), F32),
        ],
        compiler_params=_params(("parallel",), 40),
        name="outproj_router",
    )(oa, ob, x2, w_a, w_b, norm_ffn, w_router_parts)


def _topk_body(aff_ref, afft_ref, idx_ref, c_scr, *, seq, cap, n_exp):
    a = aff_ref[...]
    bits = lax.bitcast_convert_type(a, I32)
    bits_t = lax.bitcast_convert_type(afft_ref[0], I32)

    def bisect(i, lo):
        cand = lo | jnp.left_shift(jnp.int32(1), 30 - i)
        cnt = jnp.sum((bits_t >= cand).astype(I32), axis=1, keepdims=True)
        return jnp.where(cnt >= cap, cand, lo)

    thr_col = lax.fori_loop(0, 31, bisect, jnp.zeros((n_exp, 1), I32))
    diag = (lax.broadcasted_iota(I32, (n_exp, n_exp), 0)
            == lax.broadcasted_iota(I32, (n_exp, n_exp), 1))
    thr = jnp.sum(jnp.where(diag, thr_col, 0), axis=0, keepdims=True)
    gt = bits > thr
    eq = bits == thr
    need = (cap - jnp.sum(gt.astype(I32), axis=0, keepdims=True)).astype(F32)

    rc = 256
    r_i = lax.broadcasted_iota(I32, (rc, rc), 0)
    c_i = lax.broadcasted_iota(I32, (rc, rc), 1)
    lower = (c_i <= r_i).astype(BF16)

    def prefix(mask_f32):
        carry = jnp.zeros((1, n_exp), F32)
        for k in range(seq // rc):
            blk = mask_f32[k * rc:(k + 1) * rc, :]
            inc = jnp.dot(lower, blk.astype(BF16), preferred_element_type=F32) + carry
            c_scr[k * rc:(k + 1) * rc, :] = inc
            carry = inc[rc - 1:rc, :]
        return c_scr[...]

    eq_f = eq.astype(F32)
    tie_rank = prefix(eq_f) - eq_f
    sel = jnp.logical_or(gt, jnp.logical_and(eq, tie_rank < need))
    prefix(sel.astype(F32))

    r_row = lax.broadcasted_iota(I32, (1, cap), 1).astype(F32)
    rows = 512
    for e in range(n_exp):
        def count(k, acc, e=e):
            r0 = pl.multiple_of(k * rows, rows)
            col = c_scr[pl.ds(r0, rows), e:e + 1]
            hit = (col <= r_row).astype(F32)
            return acc + jnp.sum(hit.reshape(rows // SUBLANES, SUBLANES, cap), axis=0)

        acc = lax.fori_loop(0, seq // rows, count, jnp.zeros((SUBLANES, cap), F32))
        idx_ref[0, e:e + 1, :] = jnp.sum(acc, axis=0, keepdims=True).astype(I32)


def _topk(cfg, aff, aff_t):
    S, E, C = cfg.seq, cfg.n_experts, cfg.cap
    return pl.pallas_call(
        functools.partial(_topk_body, seq=S, cap=C, n_exp=E),
        grid=(cfg.batch,),
        in_specs=[pl.BlockSpec((S, E), lambda b: (b, 0)),
                  pl.BlockSpec((1, E, S), lambda b: (b, 0, 0))],
        out_specs=pl.BlockSpec((1, E, C), lambda b: (b, 0, 0)),
        out_shape=jax.ShapeDtypeStruct((cfg.batch, E, C), I32),
        scratch_shapes=[pltpu.VMEM((S, E), F32)],
        compiler_params=_params(("parallel",), 40),
        name="topk",
    )(aff, aff_t)


def _gather_body(idx_ref, h_ref, o_ref, tile_scr, *, cap, half_tiles, stride, group):
    def expert(x, carry):
        for m in range(cap):
            t = idx_ref[x, 0, m]
            row = pl.multiple_of(t * half_tiles, half_tiles)
            tile_scr[pl.ds(m, half_tiles, stride=stride), :] = h_ref[0, 0, pl.ds(row, half_tiles), :]
        for j in range(half_tiles):
            o_ref[x, :, j * LANES:(j + 1) * LANES] = tile_scr[pl.ds(j * stride, cap), :].astype(BF16)
        return carry

    lax.fori_loop(0, group, expert, 0)


def _gather(cfg, idx3, h_tm):
    B, S, E, C, D = cfg.batch, cfg.seq, cfg.n_experts, cfg.cap, cfg.d_model
    ht = D // (2 * LANES)
    stride = C + SUBLANES
    ge = GATHER_EXPERTS_PER_STEP
    return pl.pallas_call(
        functools.partial(_gather_body, cap=C, half_tiles=ht, stride=stride, group=ge),
        grid=(B, 2, E // ge),
        in_specs=[
            pl.BlockSpec((ge, 1, C), lambda b, hf, g: (b * (E // ge) + g, 0, 0), memory_space=pltpu.SMEM),
            pl.BlockSpec((1, 1, S * ht, LANES), lambda b, hf, g: (b, hf, 0, 0)),
        ],
        out_specs=pl.BlockSpec((ge, C, D // 2), lambda b, hf, g: (g, b, hf)),
        out_shape=jax.ShapeDtypeStruct((E, B * C, D), BF16),
        scratch_shapes=[pltpu.VMEM((ht * stride, LANES), F32)],
        compiler_params=_params(("parallel", "parallel", "arbitrary"), 48),
        name="gather_tokens",
    )(idx3, h_tm)


def _ffn_body(x_ref, wg_ref, wu_ref, wd_ref, o_ref, h_scr, *, tm, mc, tf, tn, half_tiles, n_f, n_n):
    fh = pl.program_id(1)
    j = pl.program_id(2)

    chunks = [slice(r, r + mc) for r in range(0, tm, mc)]

    @pl.when(jnp.logical_and(fh == 0, j == 0))
    def _():
        o_ref[...] = jnp.zeros(o_ref.shape, F32)

    @pl.when(j < n_f)
    def _():
        wg = wg_ref[0].astype(BF16)
        wu = wu_ref[0].astype(BF16)
        hid = []
        for rs in chunks:
            x = x_ref[0, rs, :]
            a = jnp.dot(x, wg, preferred_element_type=F32)
            u = jnp.dot(x, wu, preferred_element_type=F32)
            hid.append((a * jax.nn.sigmoid(a) * u).astype(BF16))
        for f in range(n_f):
            @pl.when(j == f)
            def _(f=f):
                for rs, hc in zip(chunks, hid):
                    h_scr[rs, f * tf:(f + 1) * tf] = hc

    for n in range(n_n):
        @pl.when(j == n_f + n)
        def _(n=n):
            wd = wd_ref[0].astype(BF16)
            for rs in chunks:
                y = jnp.dot(h_scr[rs, :], wd, preferred_element_type=F32)
                for c in range(tn // LANES):
                    half, row = divmod(n * (tn // LANES) + c, half_tiles)
                    rows = pl.ds(rs.start * half_tiles + row, mc, stride=half_tiles)
                    o_ref[0, half, rows, :] = o_ref[0, half, rows, :] + y[:, c * LANES:(c + 1) * LANES]


def _ffn(cfg, xg, w_gate, w_up, w_down):
    E, D, F = cfg.n_experts, cfg.d_model, cfg.d_ff
    tm = cfg.batch * cfg.cap
    fhalf = F // 2
    tf = min(512, fhalf)
    tn = 256
    ht = D // (2 * LANES)
    n_f = fhalf // tf
    n_n = D // tn
    return pl.pallas_call(
        functools.partial(_ffn_body, tm=tm, mc=min(512, tm), tf=tf, tn=tn, half_tiles=ht, n_f=n_f,
                          n_n=n_n),
        grid=(E, 2, n_f + n_n),
        in_specs=[
            _resident((1, tm, D), lambda e, h, j: (e, 0, 0)),
            pl.BlockSpec((1, D, tf), lambda e, h, j: (e, 0, h * n_f + jnp.minimum(j, n_f - 1))),
            pl.BlockSpec((1, D, tf), lambda e, h, j: (e, 0, h * n_f + jnp.minimum(j, n_f - 1))),
            pl.BlockSpec((1, fhalf, tn), lambda e, h, j: (e, h, jnp.maximum(j - n_f, 0))),
        ],
        out_specs=pl.BlockSpec((1, 2, tm * ht, LANES), lambda e, h, j: (e, 0, 0, 0),
                               pipeline_mode=pl.Buffered(1)),
        out_shape=jax.ShapeDtypeStruct((E, 2, tm * ht, LANES), F32),
        scratch_shapes=[pltpu.VMEM((tm, fhalf), BF16)],
        compiler_params=_params(("parallel", "arbitrary", "arbitrary"), 62),
        name="expert_ffn",
    )(xg, w_gate, w_up, w_down)


def _combine_body(idx_ref, gate_ref, eo_ref, o_ref, *, cap, half_tiles, unroll, group):
    @pl.when(pl.program_id(2) == 0)
    def _():
        o_ref[...] = jnp.zeros(o_ref.shape, F32)

    for x in range(group):
        def rows(g, c, x=x):
            base = g * unroll
            dsts, vals = [], []
            for u in range(unroll):
                t = idx_ref[x, 0, base + u]
                dst = pl.multiple_of(t * half_tiles, half_tiles)
                src = pl.multiple_of((base + u) * half_tiles, half_tiles)
                dsts.append(dst)
                vals.append(o_ref[0, 0, pl.ds(dst, half_tiles), :]
                            + eo_ref[x, 0, pl.ds(src, half_tiles), :] * gate_ref[x, 0, t])
            for u in range(unroll):
                o_ref[0, 0, pl.ds(dsts[u], half_tiles), :] = vals[u]
            return c

        lax.fori_loop(0, cap // unroll, rows, 0)


def _combine(cfg, idx3, gates3, eo_tm):
    B, S, E, C, D = cfg.batch, cfg.seq, cfg.n_experts, cfg.cap, cfg.d_model
    ht = D // (2 * LANES)
    ge = COMBINE_EXPERTS_PER_STEP
    return pl.pallas_call(
        functools.partial(_combine_body, cap=C, half_tiles=ht, unroll=8, group=ge),
        grid=(B, 2, E // ge),
        in_specs=[
            pl.BlockSpec((ge, 1, C), lambda b, h, g: (b * (E // ge) + g, 0, 0), memory_space=pltpu.SMEM),
            pl.BlockSpec((ge, 1, S), lambda b, h, g: (b * (E // ge) + g, 0, 0), memory_space=pltpu.SMEM),
            pl.BlockSpec((ge, 1, C * ht, LANES), lambda b, h, g: (g, h, b, 0)),
        ],
        out_specs=pl.BlockSpec((1, 1, S * ht, LANES), lambda b, h, g: (b, h, 0, 0)),
        out_shape=jax.ShapeDtypeStruct((B, 2, S * ht, LANES), F32),
        compiler_params=_params(("parallel", "parallel", "arbitrary"), 48),
        name="combine",
    )(idx3, gates3, eo_tm)


def _final_body(f_ref, x_ref, g_ref, o_ref, *, tm, half_tiles, d_model):
    ss = jnp.zeros((tm, 1), F32)
    for j in range(2 * half_tiles):
        cols = slice(j * LANES, (j + 1) * LANES)
        y = x_ref[:, cols] + f_ref[0, j // half_tiles, pl.ds(j % half_tiles, tm, stride=half_tiles), :]
        o_ref[:, cols] = y
        ss = ss + jnp.sum(y * y, axis=-1, keepdims=True)
    o_ref[...] = o_ref[...] * lax.rsqrt(ss / d_model + EPS) * g_ref[...]


def _final(cfg, ffn_tm, x1, norm_final):
    T, D = x1.shape
    ht = D // (2 * LANES)
    tm = 256
    sb = cfg.seq // tm
    return pl.pallas_call(
        functools.partial(_final_body, tm=tm, half_tiles=ht, d_model=D),
        grid=(T // tm,),
        in_specs=[
            pl.BlockSpec((1, 2, tm * ht, LANES), lambda i: (i // sb, 0, i % sb, 0)),
            pl.BlockSpec((tm, D), lambda i: (i, 0)),
            _resident((1, D), lambda i: (0, 0)),
        ],
        out_specs=pl.BlockSpec((tm, D), lambda i: (i, 0)),
        out_shape=jax.ShapeDtypeStruct((T, D), F32),
        compiler_params=_params(("parallel",), 40),
        name="final_norm",
    )(ffn_tm, x1, norm_final)


def _rope_tables(seq):
    f32 = np.float32
    rows = seq // GRID_W
    row = np.repeat(np.arange(rows), GRID_W).astype(f32)
    col = np.tile(np.arange(GRID_W), rows).astype(f32)
    half = HEAD_DIM // 2
    inv_freq = np.power(f32(ROPE_THETA), -np.arange(0, half, 2, dtype=f32) / f32(half)).astype(f32)
    ang_r = row[:, None] * inv_freq[None, :]
    ang_c = col[:, None] * inv_freq[None, :]
    cos_t = np.concatenate([np.cos(ang_r), np.cos(ang_r), np.cos(ang_c), np.cos(ang_c)], axis=-1)
    sin_t = np.concatenate([-np.sin(ang_r), np.sin(ang_r), -np.sin(ang_c), np.sin(ang_c)], axis=-1)
    return jnp.asarray(cos_t, F32), jnp.asarray(sin_t, F32)


def _layer(cfg, x2, cos_t, sin_t, slopes, norm_mix, w_in, sink_a, q_norm_b, k_norm_b, w_out,
           norm_ffn, w_router, w_gate, w_up, w_down):
    B, S, E, C = cfg.batch, cfg.seq, cfg.n_experts, cfg.cap
    qa, ka, va, qb, kb, vb = _inproj(cfg, x2, norm_mix[None], w_in.astype(BF16),
                                     q_norm_b[None], k_norm_b[None], cos_t, sin_t)
    oa = _win_attention(cfg, sink_a, slopes, qa, ka, va)
    ob = _grid_attention(cfg, qb, kb, vb)
    w_out_bf = w_out.astype(BF16)
    n_a = H_A * HEAD_DIM
    wr_hi = w_router.astype(BF16)
    wr_lo = (w_router - wr_hi.astype(F32)).astype(BF16)
    x1, h_tm, aff = _outproj(cfg, oa, ob, x2, w_out_bf[:n_a], w_out_bf[n_a:], norm_ffn[None],
                             jnp.concatenate([wr_hi, wr_lo], axis=-1))
    aff_t = aff.reshape(B, S, E).transpose(0, 2, 1)
    idx = _topk(cfg, aff, aff_t)
    idx3 = idx.reshape(B * E, 1, C)
    gates3 = aff_t.reshape(B * E, 1, S)
    xg = _gather(cfg, idx3, h_tm)
    eo_tm = _ffn(cfg, xg, w_gate, w_up, w_down)
    ffn_tm = _combine(cfg, idx3, gates3, eo_tm)
    return x1, ffn_tm


def _forward(cfg, x, norm_mix, w_in, sink_a, q_norm_b, k_norm_b, w_out, norm_ffn,
             w_router, w_gate, w_up, w_down, norm_final):
    B, S, D = x.shape
    assert norm_mix.shape[0] == 1, "single-layer stack only"
    cos_t, sin_t = _rope_tables(S)
    slopes = jnp.asarray(2.0 ** (-8.0 * np.arange(1, H_A + 1) / H_A), dtype=F32)
    x2 = x.reshape(B * S, D)
    x1, ffn_tm = _layer(cfg, x2, cos_t, sin_t, slopes, norm_mix[0], w_in[0], sink_a[0],
                        q_norm_b[0], k_norm_b[0], w_out[0], norm_ffn[0], w_router[0],
                        w_gate[0], w_up[0], w_down[0])
    out = _final(cfg, ffn_tm, x1, norm_final[None])
    return out.reshape(B, S, D)


def kernel(x, norm_mix, w_in, sink_a, q_norm_b, k_norm_b, w_out, norm_ffn, w_router, w_gate, w_up,
           w_down, norm_final):
    B, S, D = x.shape
    E = w_router.shape[-1]
    cfg = Cfg(batch=B, seq=S, d_model=D, d_ff=w_gate.shape[-1], n_experts=E, cap=2 * S // E)
    return _forward(cfg, x, norm_mix, w_in, sink_a, q_norm_b, k_norm_b, w_out, norm_ffn,
                    w_router, w_gate, w_up, w_down, norm_final)
```

```python
import functools
from typing import NamedTuple

import numpy as np
import jax
import jax.numpy as jnp
from jax import lax
from jax.experimental import pallas as pl
from jax.experimental.pallas import tpu as pltpu

HEAD_DIM = 128
H_A, KV_A = 8, 2
H_B, KV_B = 8, 2
GROUP = 4
WINDOW = 128
BLOCK = 128
GRID_W = 64
ROPE_THETA = 10000.0
CAPACITY_FACTOR = 2
EPS = 1e-6
NEG_INF = -1e30
SCALE = HEAD_DIM ** -0.5
LOG2E = 1.4426950408889634
LANES = 128
SUBLANES = 8
N_PROJ_HEADS = H_A + 2 * KV_A + H_B + 2 * KV_B
MIB = 1024 * 1024

PROJ_ROWS = 512
PROJ_ROW_CHUNK = 256
FINAL_ROWS = 512
GRID_Q_ROWS = 256
GRID_TILES_PER_ITER = 4
WIN_BLOCKS_PER_ITER = 8
TOPK_PREFIX_ROWS = 256
TOPK_COUNT_ROWS = 512
GATHER_EXPERTS_PER_STEP = 4
COMBINE_ROWS_PER_GROUP = 8
FFN_ROW_CHUNK = 512
FFN_FF_TILE = 512
FFN_OUT_TILE = 256
VMEM_MIB_DEFAULT = 40
VMEM_MIB_LARGE = 48
VMEM_MIB_FFN = 62

F32 = jnp.float32
BF16 = jnp.bfloat16
I32 = jnp.int32


class Cfg(NamedTuple):
    batch: int
    seq: int
    d_model: int
    d_ff: int
    n_experts: int
    cap: int


def _params(sem, vmem_mib):
    return pltpu.CompilerParams(dimension_semantics=sem, vmem_limit_bytes=vmem_mib * MIB)


def _resident(shape, index_map):
    return pl.BlockSpec(shape, index_map, pipeline_mode=pl.Buffered(1))


def _inproj_body(x_ref, g_ref, w_ref, qn_ref, kn_ref, cos_ref, sin_ref,
                 qa_ref, ka_ref, va_ref, qb_ref, kb_ref, vb_ref):
    xf = x_ref[...]
    ms = jnp.mean(xf * xf, axis=-1, keepdims=True)
    h = (xf * lax.rsqrt(ms + EPS) * g_ref[...]).astype(BF16)
    n_a = (H_A + 2 * KV_A) * HEAD_DIM
    proj_b = jnp.dot(h, w_ref[:, n_a:], preferred_element_type=F32)
    proj_a = jnp.dot(h, w_ref[:, :n_a], preferred_element_type=F32)
    cos = cos_ref[...]
    sin = sin_ref[...]
    lane = lax.broadcasted_iota(I32, (1, LANES), 1)
    first = (lane % 64) < 32

    def head(c):
        if c * HEAD_DIM < n_a:
            return proj_a[:, c * HEAD_DIM:(c + 1) * HEAD_DIM]
        return proj_b[:, c * HEAD_DIM - n_a:(c + 1) * HEAD_DIM - n_a]

    def norm_rope(t, gain):
        m = jnp.mean(t * t, axis=-1, keepdims=True)
        tn = t * lax.rsqrt(m + EPS) * gain
        partner = jnp.where(first, pltpu.roll(tn, 96, 1), pltpu.roll(tn, 32, 1))
        return tn * cos + partner * sin

    c = H_A + 2 * KV_A
    for hh in range(H_B):
        qb_ref[hh] = (norm_rope(head(c + hh), qn_ref[...]) * (SCALE * LOG2E)).astype(BF16)
    c += H_B
    for hh in range(KV_B):
        kb_ref[hh] = norm_rope(head(c + hh), kn_ref[...]).astype(BF16)
    c += KV_B
    for hh in range(KV_B):
        vb_ref[hh] = head(c + hh).astype(BF16)
    c = 0
    for hh in range(H_A):
        qa_ref[hh] = (head(c + hh) * (SCALE * LOG2E)).astype(BF16)
    c += H_A
    for hh in range(KV_A):
        ka_ref[hh] = head(c + hh).astype(BF16)
    c += KV_A
    for hh in range(KV_A):
        va_ref[hh] = head(c + hh).astype(BF16)


def _inproj(cfg, x2, norm_mix, w_in_bf, q_norm, k_norm, cos_t, sin_t):
    T, D = x2.shape
    tm = PROJ_ROWS
    sb = cfg.seq // tm
    heads = lambda n: jax.ShapeDtypeStruct((n, T, HEAD_DIM), BF16)
    hspec = lambda n: pl.BlockSpec((n, tm, HEAD_DIM), lambda i: (0, i, 0))
    return pl.pallas_call(
        _inproj_body,
        grid=(T // tm,),
        in_specs=[
            pl.BlockSpec((tm, D), lambda i: (i, 0)),
            _resident((1, D), lambda i: (0, 0)),
            _resident((D, N_PROJ_HEADS * HEAD_DIM), lambda i: (0, 0)),
            _resident((1, HEAD_DIM), lambda i: (0, 0)),
            _resident((1, HEAD_DIM), lambda i: (0, 0)),
            pl.BlockSpec((tm, HEAD_DIM), lambda i: (i % sb, 0)),
            pl.BlockSpec((tm, HEAD_DIM), lambda i: (i % sb, 0)),
        ],
        out_specs=[hspec(H_A), hspec(KV_A), hspec(KV_A), hspec(H_B), hspec(KV_B), hspec(KV_B)],
        out_shape=[heads(H_A), heads(KV_A), heads(KV_A), heads(H_B), heads(KV_B), heads(KV_B)],
        compiler_params=_params(("parallel",), VMEM_MIB_LARGE),
        name="inproj",
    )(x2, norm_mix, w_in_bf, q_norm, k_norm, cos_t, sin_t)


def _win_body(sink_ref, slope_ref, q_ref, k_ref, v_ref, o_ref, kt_scr, vx_scr, bias_scr, sink_scr,
              *, seq, unroll):
    kvh = pl.program_id(1)
    span = 3 * BLOCK
    rows = GROUP * BLOCK
    nb = seq // BLOCK

    kt_scr[...] = k_ref[0].astype(F32).T.astype(BF16)
    vx_scr[:, :HEAD_DIM] = v_ref[0]
    vx_scr[:, HEAD_DIM:] = jnp.ones((seq, HEAD_DIM), BF16)
    qi = lax.broadcasted_iota(I32, (BLOCK, span), 0)
    kj = lax.broadcasted_iota(I32, (BLOCK, span), 1)
    for g in range(GROUP):
        hq = kvh * GROUP + g
        sink_scr[g * BLOCK:(g + 1) * BLOCK, :] = jnp.full((BLOCK, LANES), sink_ref[hq], F32) * LOG2E
        for place in range(3):
            dist = jnp.abs(qi + place * BLOCK - kj)
            bias = jnp.where(dist <= WINDOW, (slope_ref[hq] * dist.astype(F32)) * (-LOG2E), NEG_INF)
            bias_scr[place, g * BLOCK:(g + 1) * BLOCK, :] = bias

    def block(n):
        ws = pl.multiple_of(jnp.clip((n - 1) * BLOCK, 0, seq - span), BLOCK)
        q0 = pl.multiple_of(n * BLOCK, BLOCK)
        place = jnp.where(n == 0, 0, jnp.where(n == nb - 1, 2, 1))
        q = q_ref[:, pl.ds(q0, BLOCK), :].reshape(rows, HEAD_DIM)
        s = jnp.dot(q, kt_scr[:, pl.ds(ws, span)], preferred_element_type=F32) + bias_scr[place]
        sink = sink_scr[:, :1]
        m = jnp.maximum(jnp.max(s, axis=-1, keepdims=True), sink)
        p = jnp.exp2(s - m).astype(BF16)
        ox = jnp.dot(p, vx_scr[pl.ds(ws, span), :], preferred_element_type=F32)
        o = ox[:, :HEAD_DIM] / (ox[:, HEAD_DIM:] + jnp.exp2(sink - m))
        for g in range(GROUP):
            o_ref[pl.ds(q0, BLOCK), g * HEAD_DIM:(g + 1) * HEAD_DIM] = (
                o[g * BLOCK:(g + 1) * BLOCK].astype(BF16))

    def step(i, carry):
        for u in range(unroll):
            block(i * unroll + u)
        return carry

    lax.fori_loop(0, nb // unroll, step, 0)


def _win_attention(cfg, sink, slopes, qa, ka, va):
    S = cfg.seq
    T = cfg.batch * S
    assert S // BLOCK >= 4
    smem = pl.BlockSpec(memory_space=pltpu.SMEM)
    rows = GROUP * BLOCK
    return pl.pallas_call(
        functools.partial(_win_body, seq=S, unroll=WIN_BLOCKS_PER_ITER),
        grid=(cfg.batch, KV_A),
        in_specs=[
            smem, smem,
            pl.BlockSpec((GROUP, S, HEAD_DIM), lambda b, k: (k, b, 0)),
            pl.BlockSpec((1, S, HEAD_DIM), lambda b, k: (k, b, 0)),
            pl.BlockSpec((1, S, HEAD_DIM), lambda b, k: (k, b, 0)),
        ],
        out_specs=pl.BlockSpec((S, GROUP * HEAD_DIM), lambda b, k: (b, k)),
        out_shape=jax.ShapeDtypeStruct((T, H_A * HEAD_DIM), BF16),
        scratch_shapes=[pltpu.VMEM((HEAD_DIM, S), BF16), pltpu.VMEM((S, 2 * HEAD_DIM), BF16),
                        pltpu.VMEM((3, rows, 3 * BLOCK), F32), pltpu.VMEM((rows, LANES), F32)],
        compiler_params=_params(("parallel", "parallel"), VMEM_MIB_DEFAULT),
        name="win_attention",
    )(sink, slopes, qa, ka, va)


def _grid_body(q_ref, k_ref, v_ref, o_ref, kt_scr, vx_scr, *, seq, tq, unroll):
    kt_scr[...] = k_ref[0].astype(F32).T.astype(BF16)
    vx_scr[:, :HEAD_DIM] = v_ref[0]
    vx_scr[:, HEAD_DIM:] = jnp.ones((seq, HEAD_DIM), BF16)

    def tile(g, r0):
        q = q_ref[g, pl.ds(r0, tq), :]
        s = jnp.dot(q, kt_scr[...], preferred_element_type=F32)
        m = jnp.max(s, axis=-1, keepdims=True)
        p = jnp.exp2(s - m).astype(BF16)
        ox = jnp.dot(p, vx_scr[...], preferred_element_type=F32)
        o = ox[:, :HEAD_DIM] / ox[:, HEAD_DIM:]
        o_ref[pl.ds(r0, tq), g * HEAD_DIM:(g + 1) * HEAD_DIM] = o.astype(BF16)

    def step(i, carry):
        for u in range(unroll):
            r0 = pl.multiple_of((i * unroll + u) * tq, tq)
            for g in range(GROUP):
                tile(g, r0)
        return carry

    lax.fori_loop(0, seq // (tq * unroll), step, 0)


def _grid_attention(cfg, qb, kb, vb):
    S = cfg.seq
    T = cfg.batch * S
    tq = GRID_Q_ROWS
    return pl.pallas_call(
        functools.partial(_grid_body, seq=S, tq=tq, unroll=GRID_TILES_PER_ITER),
        grid=(cfg.batch, KV_B),
        in_specs=[
            pl.BlockSpec((GROUP, S, HEAD_DIM), lambda b, k: (k, b, 0)),
            pl.BlockSpec((1, S, HEAD_DIM), lambda b, k: (k, b, 0)),
            pl.BlockSpec((1, S, HEAD_DIM), lambda b, k: (k, b, 0)),
        ],
        out_specs=pl.BlockSpec((S, GROUP * HEAD_DIM), lambda b, k: (b, k)),
        out_shape=jax.ShapeDtypeStruct((T, H_B * HEAD_DIM), BF16),
        scratch_shapes=[pltpu.VMEM((HEAD_DIM, S), BF16), pltpu.VMEM((S, 2 * HEAD_DIM), BF16)],
        compiler_params=_params(("parallel", "parallel"), VMEM_MIB_LARGE),
        name="grid_attention",
    )(qb, kb, vb)


def _outproj_body(oa_ref, ob_ref, x_ref, wa_ref, wb_ref, g_ref, wr_ref,
                  x1_ref, h_ref, aff_ref, *, tm, mc, half_tiles, n_exp):
    for c in range(tm // mc):
        rows = slice(c * mc, (c + 1) * mc)
        mix = jnp.dot(oa_ref[rows, :], wa_ref[...], preferred_element_type=F32)
        mix = mix + jnp.dot(ob_ref[rows, :], wb_ref[...], preferred_element_type=F32)
        x1 = x_ref[rows, :] + mix
        x1_ref[rows, :] = x1
        ms = jnp.mean(x1 * x1, axis=-1, keepdims=True)
        h = x1 * lax.rsqrt(ms + EPS) * g_ref[...]
        h_hi = h.astype(BF16)
        h_lo = (h - h_hi.astype(F32)).astype(BF16)
        r = (jnp.dot(h_hi, wr_ref[...], preferred_element_type=F32)
             + jnp.dot(h_lo, wr_ref[...], preferred_element_type=F32))
        logits = r[:, :n_exp] + r[:, n_exp:]
        e = jnp.exp(logits - jnp.max(logits, axis=-1, keepdims=True))
        aff_ref[rows, :] = e / jnp.sum(e, axis=-1, keepdims=True)
        for half in range(2):
            for j in range(half_tiles):
                c0 = (half * half_tiles + j) * LANES
                h_ref[0, half, pl.ds(c * mc * half_tiles + j, mc, stride=half_tiles), :] = (
                    h[:, c0:c0 + LANES])


def _outproj(cfg, oa, ob, x2, w_a, w_b, norm_ffn, w_router_parts):
    T, D = x2.shape
    E = cfg.n_experts
    S = cfg.seq
    tm = PROJ_ROWS
    sb = S // tm
    ht = D // (2 * LANES)
    return pl.pallas_call(
        functools.partial(_outproj_body, tm=tm, mc=PROJ_ROW_CHUNK, half_tiles=ht, n_exp=E),
        grid=(T // tm,),
        in_specs=[
            pl.BlockSpec((tm, oa.shape[1]), lambda i: (i, 0)),
            pl.BlockSpec((tm, ob.shape[1]), lambda i: (i, 0)),
            pl.BlockSpec((tm, D), lambda i: (i, 0)),
            _resident(w_a.shape, lambda i: (0, 0)),
            _resident(w_b.shape, lambda i: (0, 0)),
            _resident((1, D), lambda i: (0, 0)),
            _resident((D, 2 * E), lambda i: (0, 0)),
        ],
        out_specs=[
            pl.BlockSpec((tm, D), lambda i: (i, 0)),
            pl.BlockSpec((1, 2, tm * ht, LANES), lambda i: (i // sb, 0, i % sb, 0)),
            pl.BlockSpec((tm, E), lambda i: (i, 0)),
        ],
        out_shape=[
            jax.ShapeDtypeStruct((T, D), F32),
            jax.ShapeDtypeStruct((cfg.batch, 2, S * ht, LANES), F32),
            jax.ShapeDtypeStruct((T, E), F32),
        ],
        compiler_params=_params(("parallel",), VMEM_MIB_DEFAULT),
        name="outproj_router",
    )(oa, ob, x2, w_a, w_b, norm_ffn, w_router_parts)


def _topk_body(aff_ref, afft_ref, idx_ref, c_scr, *, seq, cap, n_exp):
    a = aff_ref[...]
    bits = lax.bitcast_convert_type(a, I32)
    bits_t = lax.bitcast_convert_type(afft_ref[0], I32)

    def bisect(i, lo):
        cand = lo | jnp.left_shift(jnp.int32(1), 30 - i)
        cnt = jnp.sum((bits_t >= cand).astype(I32), axis=1, keepdims=True)
        return jnp.where(cnt >= cap, cand, lo)

    thr_col = lax.fori_loop(0, 31, bisect, jnp.zeros((n_exp, 1), I32))
    diag = (lax.broadcasted_iota(I32, (n_exp, n_exp), 0)
            == lax.broadcasted_iota(I32, (n_exp, n_exp), 1))
    thr = jnp.sum(jnp.where(diag, thr_col, 0), axis=0, keepdims=True)
    gt = bits > thr
    eq = bits == thr
    need = (cap - jnp.sum(gt.astype(I32), axis=0, keepdims=True)).astype(F32)

    rc = TOPK_PREFIX_ROWS
    r_i = lax.broadcasted_iota(I32, (rc, rc), 0)
    c_i = lax.broadcasted_iota(I32, (rc, rc), 1)
    lower = (c_i <= r_i).astype(BF16)

    def prefix(mask_f32):
        carry = jnp.zeros((1, n_exp), F32)
        for k in range(seq // rc):
            blk = mask_f32[k * rc:(k + 1) * rc, :]
            inc = jnp.dot(lower, blk.astype(BF16), preferred_element_type=F32) + carry
            c_scr[k * rc:(k + 1) * rc, :] = inc
            carry = inc[rc - 1:rc, :]
        return c_scr[...]

    eq_f = eq.astype(F32)
    tie_rank = prefix(eq_f) - eq_f
    sel = jnp.logical_or(gt, jnp.logical_and(eq, tie_rank < need))
    prefix(sel.astype(F32))

    r_row = lax.broadcasted_iota(I32, (1, cap), 1).astype(F32)
    rows = TOPK_COUNT_ROWS
    for e in range(n_exp):
        def count(k, acc, e=e):
            r0 = pl.multiple_of(k * rows, rows)
            col = c_scr[pl.ds(r0, rows), e:e + 1]
            hit = (col <= r_row).astype(F32)
            return acc + jnp.sum(hit.reshape(rows // SUBLANES, SUBLANES, cap), axis=0)

        acc = lax.fori_loop(0, seq // rows, count, jnp.zeros((SUBLANES, cap), F32))
        idx_ref[0, e:e + 1, :] = jnp.sum(acc, axis=0, keepdims=True).astype(I32)


def _topk(cfg, aff, aff_t):
    S, E, C = cfg.seq, cfg.n_experts, cfg.cap
    return pl.pallas_call(
        functools.partial(_topk_body, seq=S, cap=C, n_exp=E),
        grid=(cfg.batch,),
        in_specs=[pl.BlockSpec((S, E), lambda b: (b, 0)),
                  pl.BlockSpec((1, E, S), lambda b: (b, 0, 0))],
        out_specs=pl.BlockSpec((1, E, C), lambda b: (b, 0, 0)),
        out_shape=jax.ShapeDtypeStruct((cfg.batch, E, C), I32),
        scratch_shapes=[pltpu.VMEM((S, E), F32)],
        compiler_params=_params(("parallel",), VMEM_MIB_DEFAULT),
        name="topk",
    )(aff, aff_t)


def _gather_body(idx_ref, h_ref, o_ref, tile_scr, *, cap, half_tiles, stride, group):
    def expert(x, carry):
        for m in range(cap):
            t = idx_ref[x, 0, m]
            row = pl.multiple_of(t * half_tiles, half_tiles)
            tile_scr[pl.ds(m, half_tiles, stride=stride), :] = h_ref[0, 0, pl.ds(row, half_tiles), :]
        for j in range(half_tiles):
            o_ref[x, :, j * LANES:(j + 1) * LANES] = tile_scr[pl.ds(j * stride, cap), :].astype(BF16)
        return carry

    lax.fori_loop(0, group, expert, 0)


def _gather(cfg, idx3, h_tm):
    B, S, E, C, D = cfg.batch, cfg.seq, cfg.n_experts, cfg.cap, cfg.d_model
    ht = D // (2 * LANES)
    stride = C + SUBLANES
    ge = GATHER_EXPERTS_PER_STEP
    return pl.pallas_call(
        functools.partial(_gather_body, cap=C, half_tiles=ht, stride=stride, group=ge),
        grid=(B, 2, E // ge),
        in_specs=[
            pl.BlockSpec((ge, 1, C), lambda b, hf, g: (b * (E // ge) + g, 0, 0), memory_space=pltpu.SMEM),
            pl.BlockSpec((1, 1, S * ht, LANES), lambda b, hf, g: (b, hf, 0, 0)),
        ],
        out_specs=pl.BlockSpec((ge, C, D // 2), lambda b, hf, g: (g, b, hf)),
        out_shape=jax.ShapeDtypeStruct((E, B * C, D), BF16),
        scratch_shapes=[pltpu.VMEM((ht * stride, LANES), F32)],
        compiler_params=_params(("parallel", "parallel", "arbitrary"), VMEM_MIB_LARGE),
        name="gather_tokens",
    )(idx3, h_tm)


def _ffn_body(x_ref, wg_ref, wu_ref, wd_ref, o_ref, h_scr, *, tm, mc, tf, tn, half_tiles, n_f, n_n):
    fh = pl.program_id(1)
    j = pl.program_id(2)

    chunks = [slice(r, r + mc) for r in range(0, tm, mc)]

    @pl.when(jnp.logical_and(fh == 0, j == 0))
    def _():
        o_ref[...] = jnp.zeros(o_ref.shape, F32)

    @pl.when(j < n_f)
    def _():
        wg = wg_ref[0].astype(BF16)
        wu = wu_ref[0].astype(BF16)
        hid = []
        for rs in chunks:
            x = x_ref[0, rs, :]
            a = jnp.dot(x, wg, preferred_element_type=F32)
            u = jnp.dot(x, wu, preferred_element_type=F32)
            hid.append((a * jax.nn.sigmoid(a) * u).astype(BF16))
        for f in range(n_f):
            @pl.when(j == f)
            def _(f=f):
                for rs, hc in zip(chunks, hid):
                    h_scr[rs, f * tf:(f + 1) * tf] = hc

    for n in range(n_n):
        @pl.when(j == n_f + n)
        def _(n=n):
            wd = wd_ref[0].astype(BF16)
            for rs in chunks:
                y = jnp.dot(h_scr[rs, :], wd, preferred_element_type=F32)
                for c in range(tn // LANES):
                    half, row = divmod(n * (tn // LANES) + c, half_tiles)
                    rows = pl.ds(rs.start * half_tiles + row, mc, stride=half_tiles)
                    o_ref[0, half, rows, :] = o_ref[0, half, rows, :] + y[:, c * LANES:(c + 1) * LANES]


def _ffn(cfg, xg, w_gate, w_up, w_down):
    E, D, F = cfg.n_experts, cfg.d_model, cfg.d_ff
    tm = cfg.batch * cfg.cap
    fhalf = F // 2
    tf = min(FFN_FF_TILE, fhalf)
    tn = FFN_OUT_TILE
    ht = D // (2 * LANES)
    n_f = fhalf // tf
    n_n = D // tn
    return pl.pallas_call(
        functools.partial(_ffn_body, tm=tm, mc=min(FFN_ROW_CHUNK, tm), tf=tf, tn=tn, half_tiles=ht,
                          n_f=n_f, n_n=n_n),
        grid=(E, 2, n_f + n_n),
        in_specs=[
            _resident((1, tm, D), lambda e, h, j: (e, 0, 0)),
            pl.BlockSpec((1, D, tf), lambda e, h, j: (e, 0, h * n_f + jnp.minimum(j, n_f - 1))),
            pl.BlockSpec((1, D, tf), lambda e, h, j: (e, 0, h * n_f + jnp.minimum(j, n_f - 1))),
            pl.BlockSpec((1, fhalf, tn), lambda e, h, j: (e, h, jnp.maximum(j - n_f, 0))),
        ],
        out_specs=pl.BlockSpec((1, 2, tm * ht, LANES), lambda e, h, j: (e, 0, 0, 0),
                               pipeline_mode=pl.Buffered(1)),
        out_shape=jax.ShapeDtypeStruct((E, 2, tm * ht, LANES), F32),
        scratch_shapes=[pltpu.VMEM((tm, fhalf), BF16)],
        compiler_params=_params(("parallel", "arbitrary", "arbitrary"), VMEM_MIB_FFN),
        name="expert_ffn",
    )(xg, w_gate, w_up, w_down)


def _combine_body(idx_ref, gate_ref, eo_ref, o_ref, *, cap, half_tiles, unroll):
    @pl.when(pl.program_id(2) == 0)
    def _():
        o_ref[...] = jnp.zeros(o_ref.shape, F32)

    def rows(g, carry):
        base = g * unroll
        dsts, vals = [], []
        for u in range(unroll):
            t = idx_ref[0, 0, base + u]
            dst = pl.multiple_of(t * half_tiles, half_tiles)
            src = pl.multiple_of((base + u) * half_tiles, half_tiles)
            dsts.append(dst)
            vals.append(o_ref[0, 0, pl.ds(dst, half_tiles), :]
                        + eo_ref[0, 0, pl.ds(src, half_tiles), :] * gate_ref[0, 0, t])
        for u in range(unroll):
            o_ref[0, 0, pl.ds(dsts[u], half_tiles), :] = vals[u]
        return carry

    lax.fori_loop(0, cap // unroll, rows, 0)


def _combine(cfg, idx3, gates3, eo_tm):
    B, S, E, C, D = cfg.batch, cfg.seq, cfg.n_experts, cfg.cap, cfg.d_model
    ht = D // (2 * LANES)
    return pl.pallas_call(
        functools.partial(_combine_body, cap=C, half_tiles=ht, unroll=COMBINE_ROWS_PER_GROUP),
        grid=(B, 2, E),
        in_specs=[
            pl.BlockSpec((1, 1, C), lambda b, h, e: (b * E + e, 0, 0), memory_space=pltpu.SMEM),
            pl.BlockSpec((1, 1, S), lambda b, h, e: (b * E + e, 0, 0), memory_space=pltpu.SMEM),
            pl.BlockSpec((1, 1, C * ht, LANES), lambda b, h, e: (e, h, b, 0)),
        ],
        out_specs=pl.BlockSpec((1, 1, S * ht, LANES), lambda b, h, e: (b, h, 0, 0)),
        out_shape=jax.ShapeDtypeStruct((B, 2, S * ht, LANES), F32),
        compiler_params=_params(("parallel", "parallel", "arbitrary"), VMEM_MIB_LARGE),
        name="combine",
    )(idx3, gates3, eo_tm)


def _final_body(f_ref, x_ref, g_ref, o_ref, *, tm, half_tiles, d_model):
    ss = jnp.zeros((tm, 1), F32)
    for j in range(2 * half_tiles):
        cols = slice(j * LANES, (j + 1) * LANES)
        y = x_ref[:, cols] + f_ref[0, j // half_tiles, pl.ds(j % half_tiles, tm, stride=half_tiles), :]
        o_ref[:, cols] = y
        ss = ss + jnp.sum(y * y, axis=-1, keepdims=True)
    o_ref[...] = o_ref[...] * lax.rsqrt(ss / d_model + EPS) * g_ref[...]


def _final(cfg, ffn_tm, x1, norm_final):
    T, D = x1.shape
    ht = D // (2 * LANES)
    tm = FINAL_ROWS
    sb = cfg.seq // tm
    return pl.pallas_call(
        functools.partial(_final_body, tm=tm, half_tiles=ht, d_model=D),
        grid=(T // tm,),
        in_specs=[
            pl.BlockSpec((1, 2, tm * ht, LANES), lambda i: (i // sb, 0, i % sb, 0)),
            pl.BlockSpec((tm, D), lambda i: (i, 0)),
            _resident((1, D), lambda i: (0, 0)),
        ],
        out_specs=pl.BlockSpec((tm, D), lambda i: (i, 0)),
        out_shape=jax.ShapeDtypeStruct((T, D), F32),
        compiler_params=_params(("parallel",), VMEM_MIB_DEFAULT),
        name="final_norm",
    )(ffn_tm, x1, norm_final)


def _rope_tables(seq):
    f32 = np.float32
    rows = seq // GRID_W
    row = np.repeat(np.arange(rows), GRID_W).astype(f32)
    col = np.tile(np.arange(GRID_W), rows).astype(f32)
    half = HEAD_DIM // 2
    inv_freq = np.power(f32(ROPE_THETA), -np.arange(0, half, 2, dtype=f32) / f32(half)).astype(f32)
    ang_r = row[:, None] * inv_freq[None, :]
    ang_c = col[:, None] * inv_freq[None, :]
    cos_t = np.concatenate([np.cos(ang_r), np.cos(ang_r), np.cos(ang_c), np.cos(ang_c)], axis=-1)
    sin_t = np.concatenate([-np.sin(ang_r), np.sin(ang_r), -np.sin(ang_c), np.sin(ang_c)], axis=-1)
    return jnp.asarray(cos_t, F32), jnp.asarray(sin_t, F32)


def _layer(cfg, x2, cos_t, sin_t, slopes, norm_mix, w_in, sink_a, q_norm_b, k_norm_b, w_out,
           norm_ffn, w_router, w_gate, w_up, w_down):
    B, S, E, C = cfg.batch, cfg.seq, cfg.n_experts, cfg.cap
    qa, ka, va, qb, kb, vb = _inproj(cfg, x2, norm_mix[None], w_in.astype(BF16),
                                     q_norm_b[None], k_norm_b[None], cos_t, sin_t)
    oa = _win_attention(cfg, sink_a, slopes, qa, ka, va)
    ob = _grid_attention(cfg, qb, kb, vb)
    w_out_bf = w_out.astype(BF16)
    n_a = H_A * HEAD_DIM
    wr_hi = w_router.astype(BF16)
    wr_lo = (w_router - wr_hi.astype(F32)).astype(BF16)
    x1, h_tm, aff = _outproj(cfg, oa, ob, x2, w_out_bf[:n_a], w_out_bf[n_a:], norm_ffn[None],
                             jnp.concatenate([wr_hi, wr_lo], axis=-1))
    aff_t = aff.reshape(B, S, E).transpose(0, 2, 1)
    idx = _topk(cfg, aff, aff_t)
    idx3 = idx.reshape(B * E, 1, C)
    gates3 = aff_t.reshape(B * E, 1, S)
    xg = _gather(cfg, idx3, h_tm)
    eo_tm = _ffn(cfg, xg, w_gate, w_up, w_down)
    ffn_tm = _combine(cfg, idx3, gates3, eo_tm)
    return x1, ffn_tm


def _forward(cfg, x, norm_mix, w_in, sink_a, q_norm_b, k_norm_b, w_out, norm_ffn,
             w_router, w_gate, w_up, w_down, norm_final):
    B, S, D = x.shape
    assert norm_mix.shape[0] == 1, "single-layer stack only"
    cos_t, sin_t = _rope_tables(S)
    slopes = jnp.asarray(2.0 ** (-8.0 * np.arange(1, H_A + 1) / H_A), dtype=F32)
    x2 = x.reshape(B * S, D)
    x1, ffn_tm = _layer(cfg, x2, cos_t, sin_t, slopes, norm_mix[0], w_in[0], sink_a[0],
                        q_norm_b[0], k_norm_b[0], w_out[0], norm_ffn[0], w_router[0],
                        w_gate[0], w_up[0], w_down[0])
    out = _final(cfg, ffn_tm, x1, norm_final[None])
    return out.reshape(B, S, D)


def kernel(x, norm_mix, w_in, sink_a, q_norm_b, k_norm_b, w_out, norm_ffn, w_router, w_gate, w_up,
           w_down, norm_final):
    B, S, D = x.shape
    E = w_router.shape[-1]
    cfg = Cfg(batch=B, seq=S, d_model=D, d_ff=w_gate.shape[-1], n_experts=E,
              cap=CAPACITY_FACTOR * S // E)
    return _forward(cfg, x, norm_mix, w_in, sink_a, q_norm_b, k_norm_b, w_out, norm_ffn,
                    w_router, w_gate, w_up, w_down, norm_final)
```

```python
import functools
from typing import NamedTuple

import numpy as np
import jax
import jax.numpy as jnp
from jax import lax
from jax.experimental import pallas as pl
from jax.experimental.pallas import tpu as pltpu

HEAD_DIM = 128
H_A, KV_A = 8, 2
H_B, KV_B = 8, 2
GROUP = 4
WINDOW = 128
BLOCK = 128
GRID_W = 64
ROPE_THETA = 10000.0
CAPACITY_FACTOR = 2
EPS = 1e-6
NEG_INF = -1e30
SCALE = HEAD_DIM ** -0.5
LOG2E = 1.4426950408889634
LANES = 128
SUBLANES = 8
N_PROJ_HEADS = H_A + 2 * KV_A + H_B + 2 * KV_B
MIB = 1024 * 1024

PROJ_ROWS = 512
PROJ_ROW_CHUNK = 256
FINAL_ROWS = 512
GRID_Q_ROWS = 256
GRID_TILES_PER_ITER = 4
WIN_BLOCKS_PER_ITER = 8
TOPK_PREFIX_ROWS = 256
TOPK_COUNT_ROWS = 512
GATHER_EXPERTS_PER_STEP = 4
COMBINE_ROWS_PER_GROUP = 8
FFN_ROW_CHUNK = 512
FFN_FF_TILE = 512
FFN_OUT_TILE = 256
VMEM_MIB_DEFAULT = 40
VMEM_MIB_LARGE = 48
VMEM_MIB_FFN = 62

F32 = jnp.float32
BF16 = jnp.bfloat16
I32 = jnp.int32


class Cfg(NamedTuple):
    batch: int
    seq: int
    d_model: int
    d_ff: int
    n_experts: int
    cap: int


def _params(sem, vmem_mib):
    return pltpu.CompilerParams(dimension_semantics=sem, vmem_limit_bytes=vmem_mib * MIB)


def _resident(shape, index_map):
    return pl.BlockSpec(shape, index_map, pipeline_mode=pl.Buffered(1))


def _inproj_body(x_ref, g_ref, w_ref, qn_ref, kn_ref, cos_ref, sin_ref,
                 qa_ref, ka_ref, va_ref, qb_ref, kb_ref, vb_ref):
    xf = x_ref[...]
    ms = jnp.mean(xf * xf, axis=-1, keepdims=True)
    h = (xf * lax.rsqrt(ms + EPS) * g_ref[...]).astype(BF16)
    n_a = (H_A + 2 * KV_A) * HEAD_DIM
    proj_b = jnp.dot(h, w_ref[:, n_a:], preferred_element_type=F32)
    proj_a = jnp.dot(h, w_ref[:, :n_a], preferred_element_type=F32)
    cos = cos_ref[...]
    sin = sin_ref[...]
    lane = lax.broadcasted_iota(I32, (1, LANES), 1)
    first = (lane % 64) < 32

    def head(c):
        if c * HEAD_DIM < n_a:
            return proj_a[:, c * HEAD_DIM:(c + 1) * HEAD_DIM]
        return proj_b[:, c * HEAD_DIM - n_a:(c + 1) * HEAD_DIM - n_a]

    def norm_rope(t, gain):
        m = jnp.mean(t * t, axis=-1, keepdims=True)
        tn = t * lax.rsqrt(m + EPS) * gain
        partner = jnp.where(first, pltpu.roll(tn, 96, 1), pltpu.roll(tn, 32, 1))
        return tn * cos + partner * sin

    c = H_A + 2 * KV_A
    for hh in range(H_B):
        qb_ref[hh] = (norm_rope(head(c + hh), qn_ref[...]) * (SCALE * LOG2E)).astype(BF16)
    c += H_B
    for hh in range(KV_B):
        kb_ref[hh] = norm_rope(head(c + hh), kn_ref[...]).astype(BF16)
    c += KV_B
    for hh in range(KV_B):
        vb_ref[hh] = head(c + hh).astype(BF16)
    c = 0
    for hh in range(H_A):
        qa_ref[hh] = (head(c + hh) * (SCALE * LOG2E)).astype(BF16)
    c += H_A
    for hh in range(KV_A):
        ka_ref[hh] = head(c + hh).astype(BF16)
    c += KV_A
    for hh in range(KV_A):
        va_ref[hh] = head(c + hh).astype(BF16)


def _inproj(cfg, x2, norm_mix, w_in_bf, q_norm, k_norm, cos_t, sin_t):
    T, D = x2.shape
    tm = PROJ_ROWS
    sb = cfg.seq // tm
    heads = lambda n: jax.ShapeDtypeStruct((n, T, HEAD_DIM), BF16)
    hspec = lambda n: pl.BlockSpec((n, tm, HEAD_DIM), lambda i: (0, i, 0))
    return pl.pallas_call(
        _inproj_body,
        grid=(T // tm,),
        in_specs=[
            pl.BlockSpec((tm, D), lambda i: (i, 0)),
            _resident((1, D), lambda i: (0, 0)),
            _resident((D, N_PROJ_HEADS * HEAD_DIM), lambda i: (0, 0)),
            _resident((1, HEAD_DIM), lambda i: (0, 0)),
            _resident((1, HEAD_DIM), lambda i: (0, 0)),
            pl.BlockSpec((tm, HEAD_DIM), lambda i: (i % sb, 0)),
            pl.BlockSpec((tm, HEAD_DIM), lambda i: (i % sb, 0)),
        ],
        out_specs=[hspec(H_A), hspec(KV_A), hspec(KV_A), hspec(H_B), hspec(KV_B), hspec(KV_B)],
        out_shape=[heads(H_A), heads(KV_A), heads(KV_A), heads(H_B), heads(KV_B), heads(KV_B)],
        compiler_params=_params(("parallel",), VMEM_MIB_LARGE),
        name="inproj",
    )(x2, norm_mix, w_in_bf, q_norm, k_norm, cos_t, sin_t)


def _win_body(sink_ref, slope_ref, q_ref, k_ref, v_ref, o_ref, kt_scr, vx_scr, bias_scr, sink_scr,
              *, seq, unroll):
    kvh = pl.program_id(1)
    span = 3 * BLOCK
    rows = GROUP * BLOCK
    nb = seq // BLOCK

    kt_scr[...] = k_ref[0].astype(F32).T.astype(BF16)
    vx_scr[:, :HEAD_DIM] = v_ref[0]
    vx_scr[:, HEAD_DIM:] = jnp.ones((seq, HEAD_DIM), BF16)
    qi = lax.broadcasted_iota(I32, (BLOCK, span), 0)
    kj = lax.broadcasted_iota(I32, (BLOCK, span), 1)
    for g in range(GROUP):
        hq = kvh * GROUP + g
        sink_scr[g * BLOCK:(g + 1) * BLOCK, :] = jnp.full((BLOCK, LANES), sink_ref[hq], F32) * LOG2E
        for place in range(3):
            dist = jnp.abs(qi + place * BLOCK - kj)
            bias = jnp.where(dist <= WINDOW, (slope_ref[hq] * dist.astype(F32)) * (-LOG2E), NEG_INF)
            bias_scr[place, g * BLOCK:(g + 1) * BLOCK, :] = bias

    def block(n):
        ws = pl.multiple_of(jnp.clip((n - 1) * BLOCK, 0, seq - span), BLOCK)
        q0 = pl.multiple_of(n * BLOCK, BLOCK)
        place = jnp.where(n == 0, 0, jnp.where(n == nb - 1, 2, 1))
        q = q_ref[:, pl.ds(q0, BLOCK), :].reshape(rows, HEAD_DIM)
        s = jnp.dot(q, kt_scr[:, pl.ds(ws, span)], preferred_element_type=F32) + bias_scr[place]
        sink = sink_scr[:, :1]
        m = jnp.maximum(jnp.max(s, axis=-1, keepdims=True), sink)
        p = jnp.exp2(s - m).astype(BF16)
        ox = jnp.dot(p, vx_scr[pl.ds(ws, span), :], preferred_element_type=F32)
        o = ox[:, :HEAD_DIM] / (ox[:, HEAD_DIM:] + jnp.exp2(sink - m))
        for g in range(GROUP):
            o_ref[pl.ds(q0, BLOCK), g * HEAD_DIM:(g + 1) * HEAD_DIM] = (
                o[g * BLOCK:(g + 1) * BLOCK].astype(BF16))

    def step(i, carry):
        for u in range(unroll):
            block(i * unroll + u)
        return carry

    lax.fori_loop(0, nb // unroll, step, 0)


def _win_attention(cfg, sink, slopes, qa, ka, va):
    S = cfg.seq
    T = cfg.batch * S
    assert S // BLOCK >= 4
    smem = pl.BlockSpec(memory_space=pltpu.SMEM)
    rows = GROUP * BLOCK
    return pl.pallas_call(
        functools.partial(_win_body, seq=S, unroll=WIN_BLOCKS_PER_ITER),
        grid=(cfg.batch, KV_A),
        in_specs=[
            smem, smem,
            pl.BlockSpec((GROUP, S, HEAD_DIM), lambda b, k: (k, b, 0)),
            pl.BlockSpec((1, S, HEAD_DIM), lambda b, k: (k, b, 0)),
            pl.BlockSpec((1, S, HEAD_DIM), lambda b, k: (k, b, 0)),
        ],
        out_specs=pl.BlockSpec((S, GROUP * HEAD_DIM), lambda b, k: (b, k)),
        out_shape=jax.ShapeDtypeStruct((T, H_A * HEAD_DIM), BF16),
        scratch_shapes=[pltpu.VMEM((HEAD_DIM, S), BF16), pltpu.VMEM((S, 2 * HEAD_DIM), BF16),
                        pltpu.VMEM((3, rows, 3 * BLOCK), F32), pltpu.VMEM((rows, LANES), F32)],
        compiler_params=_params(("parallel", "parallel"), VMEM_MIB_DEFAULT),
        name="win_attention",
    )(sink, slopes, qa, ka, va)


def _grid_body(q_ref, k_ref, v_ref, o_ref, kt_scr, vx_scr, *, seq, tq, unroll):
    kt_scr[...] = k_ref[0].astype(F32).T.astype(BF16)
    vx_scr[:, :HEAD_DIM] = v_ref[0]
    vx_scr[:, HEAD_DIM:] = jnp.ones((seq, HEAD_DIM), BF16)

    def tile(g, r0):
        q = q_ref[g, pl.ds(r0, tq), :]
        s = jnp.dot(q, kt_scr[...], preferred_element_type=F32)
        m = jnp.max(s, axis=-1, keepdims=True)
        p = jnp.exp2(s - m).astype(BF16)
        ox = jnp.dot(p, vx_scr[...], preferred_element_type=F32)
        o = ox[:, :HEAD_DIM] / ox[:, HEAD_DIM:]
        o_ref[pl.ds(r0, tq), g * HEAD_DIM:(g + 1) * HEAD_DIM] = o.astype(BF16)

    def step(i, carry):
        for u in range(unroll):
            r0 = pl.multiple_of((i * unroll + u) * tq, tq)
            for g in range(GROUP):
                tile(g, r0)
        return carry

    lax.fori_loop(0, seq // (tq * unroll), step, 0)


def _grid_attention(cfg, qb, kb, vb):
    S = cfg.seq
    T = cfg.batch * S
    tq = GRID_Q_ROWS
    return pl.pallas_call(
        functools.partial(_grid_body, seq=S, tq=tq, unroll=GRID_TILES_PER_ITER),
        grid=(cfg.batch, KV_B),
        in_specs=[
            pl.BlockSpec((GROUP, S, HEAD_DIM), lambda b, k: (k, b, 0)),
            pl.BlockSpec((1, S, HEAD_DIM), lambda b, k: (k, b, 0)),
            pl.BlockSpec((1, S, HEAD_DIM), lambda b, k: (k, b, 0)),
        ],
        out_specs=pl.BlockSpec((S, GROUP * HEAD_DIM), lambda b, k: (b, k)),
        out_shape=jax.ShapeDtypeStruct((T, H_B * HEAD_DIM), BF16),
        scratch_shapes=[pltpu.VMEM((HEAD_DIM, S), BF16), pltpu.VMEM((S, 2 * HEAD_DIM), BF16)],
        compiler_params=_params(("parallel", "parallel"), VMEM_MIB_LARGE),
        name="grid_attention",
    )(qb, kb, vb)


def _outproj_body(oa_ref, ob_ref, x_ref, wa_ref, wb_ref, g_ref, wr_ref,
                  x1_ref, h_ref, aff_ref, *, tm, mc, half_tiles, n_exp):
    for c in range(tm // mc):
        rows = slice(c * mc, (c + 1) * mc)
        mix = jnp.dot(oa_ref[rows, :], wa_ref[...], preferred_element_type=F32)
        mix = mix + jnp.dot(ob_ref[rows, :], wb_ref[...], preferred_element_type=F32)
        x1 = x_ref[rows, :] + mix
        x1_ref[rows, :] = x1
        ms = jnp.mean(x1 * x1, axis=-1, keepdims=True)
        h = x1 * lax.rsqrt(ms + EPS) * g_ref[...]
        h_hi = h.astype(BF16)
        h_lo = (h - h_hi.astype(F32)).astype(BF16)
        r = (jnp.dot(h_hi, wr_ref[...], preferred_element_type=F32)
             + jnp.dot(h_lo, wr_ref[...], preferred_element_type=F32))
        logits = r[:, :n_exp] + r[:, n_exp:]
        e = jnp.exp(logits - jnp.max(logits, axis=-1, keepdims=True))
        aff_ref[rows, :] = e / jnp.sum(e, axis=-1, keepdims=True)
        for half in range(2):
            for j in range(half_tiles):
                c0 = (half * half_tiles + j) * LANES
                h_ref[0, half, pl.ds(c * mc * half_tiles + j, mc, stride=half_tiles), :] = (
                    h[:, c0:c0 + LANES])


def _outproj(cfg, oa, ob, x2, w_a, w_b, norm_ffn, w_router_parts):
    T, D = x2.shape
    E = cfg.n_experts
    S = cfg.seq
    tm = PROJ_ROWS
    sb = S // tm
    ht = D // (2 * LANES)
    return pl.pallas_call(
        functools.partial(_outproj_body, tm=tm, mc=PROJ_ROW_CHUNK, half_tiles=ht, n_exp=E),
        grid=(T // tm,),
        in_specs=[
            pl.BlockSpec((tm, oa.shape[1]), lambda i: (i, 0)),
            pl.BlockSpec((tm, ob.shape[1]), lambda i: (i, 0)),
            pl.BlockSpec((tm, D), lambda i: (i, 0)),
            _resident(w_a.shape, lambda i: (0, 0)),
            _resident(w_b.shape, lambda i: (0, 0)),
            _resident((1, D), lambda i: (0, 0)),
            _resident((D, 2 * E), lambda i: (0, 0)),
        ],
        out_specs=[
            pl.BlockSpec((tm, D), lambda i: (i, 0)),
            pl.BlockSpec((1, 2, tm * ht, LANES), lambda i: (i // sb, 0, i % sb, 0)),
            pl.BlockSpec((tm, E), lambda i: (i, 0)),
        ],
        out_shape=[
            jax.ShapeDtypeStruct((T, D), F32),
            jax.ShapeDtypeStruct((cfg.batch, 2, S * ht, LANES), F32),
            jax.ShapeDtypeStruct((T, E), F32),
        ],
        compiler_params=_params(("parallel",), VMEM_MIB_DEFAULT),
        name="outproj_router",
    )(oa, ob, x2, w_a, w_b, norm_ffn, w_router_parts)


def _topk_body(aff_ref, afft_ref, idx_ref, c_scr, *, seq, cap, n_exp):
    a = aff_ref[...]
    bits = lax.bitcast_convert_type(a, I32)
    bits_t = lax.bitcast_convert_type(afft_ref[0], I32)

    def bisect(i, lo):
        cand = lo | jnp.left_shift(jnp.int32(1), 30 - i)
        cnt = jnp.sum((bits_t >= cand).astype(I32), axis=1, keepdims=True)
        return jnp.where(cnt >= cap, cand, lo)

    thr_col = lax.fori_loop(0, 31, bisect, jnp.zeros((n_exp, 1), I32))
    diag = (lax.broadcasted_iota(I32, (n_exp, n_exp), 0)
            == lax.broadcasted_iota(I32, (n_exp, n_exp), 1))
    thr = jnp.sum(jnp.where(diag, thr_col, 0), axis=0, keepdims=True)
    gt = bits > thr
    eq = bits == thr
    need = (cap - jnp.sum(gt.astype(I32), axis=0, keepdims=True)).astype(F32)

    rc = TOPK_PREFIX_ROWS
    r_i = lax.broadcasted_iota(I32, (rc, rc), 0)
    c_i = lax.broadcasted_iota(I32, (rc, rc), 1)
    lower = (c_i <= r_i).astype(BF16)

    def prefix(mask_f32):
        carry = jnp.zeros((1, n_exp), F32)
        for k in range(seq // rc):
            blk = mask_f32[k * rc:(k + 1) * rc, :]
            inc = jnp.dot(lower, blk.astype(BF16), preferred_element_type=F32) + carry
            c_scr[k * rc:(k + 1) * rc, :] = inc
            carry = inc[rc - 1:rc, :]
        return c_scr[...]

    eq_f = eq.astype(F32)
    tie_rank = prefix(eq_f) - eq_f
    sel = jnp.logical_or(gt, jnp.logical_and(eq, tie_rank < need))
    prefix(sel.astype(F32))

    r_row = lax.broadcasted_iota(I32, (1, cap), 1).astype(F32)
    rows = TOPK_COUNT_ROWS
    for e in range(n_exp):
        def count(k, acc, e=e):
            r0 = pl.multiple_of(k * rows, rows)
            col = c_scr[pl.ds(r0, rows), e:e + 1]
            hit = (col <= r_row).astype(F32)
            return acc + jnp.sum(hit.reshape(rows // SUBLANES, SUBLANES, cap), axis=0)

        acc = lax.fori_loop(0, seq // rows, count, jnp.zeros((SUBLANES, cap), F32))
        idx_ref[0, e:e + 1, :] = jnp.sum(acc, axis=0, keepdims=True).astype(I32)


def _topk(cfg, aff, aff_t):
    S, E, C = cfg.seq, cfg.n_experts, cfg.cap
    return pl.pallas_call(
        functools.partial(_topk_body, seq=S, cap=C, n_exp=E),
        grid=(cfg.batch,),
        in_specs=[pl.BlockSpec((S, E), lambda b: (b, 0)),
                  pl.BlockSpec((1, E, S), lambda b: (b, 0, 0))],
        out_specs=pl.BlockSpec((1, E, C), lambda b: (b, 0, 0)),
        out_shape=jax.ShapeDtypeStruct((cfg.batch, E, C), I32),
        scratch_shapes=[pltpu.VMEM((S, E), F32)],
        compiler_params=_params(("parallel",), VMEM_MIB_DEFAULT),
        name="topk",
    )(aff, aff_t)


def _gather_body(idx_ref, h_ref, o_ref, tile_scr, *, cap, half_tiles, stride, group):
    def expert(x, carry):
        for m in range(cap):
            t = idx_ref[x, 0, m]
            row = pl.multiple_of(t * half_tiles, half_tiles)
            tile_scr[pl.ds(m, half_tiles, stride=stride), :] = h_ref[0, 0, pl.ds(row, half_tiles), :]
        for j in range(half_tiles):
            o_ref[x, :, j * LANES:(j + 1) * LANES] = tile_scr[pl.ds(j * stride, cap), :].astype(BF16)
        return carry

    lax.fori_loop(0, group, expert, 0)


def _gather(cfg, idx3, h_tm):
    B, S, E, C, D = cfg.batch, cfg.seq, cfg.n_experts, cfg.cap, cfg.d_model
    ht = D // (2 * LANES)
    stride = C + SUBLANES
    ge = GATHER_EXPERTS_PER_STEP
    return pl.pallas_call(
        functools.partial(_gather_body, cap=C, half_tiles=ht, stride=stride, group=ge),
        grid=(B, 2, E // ge),
        in_specs=[
            pl.BlockSpec((ge, 1, C), lambda b, hf, g: (b * (E // ge) + g, 0, 0), memory_space=pltpu.SMEM),
            pl.BlockSpec((1, 1, S * ht, LANES), lambda b, hf, g: (b, hf, 0, 0)),
        ],
        out_specs=pl.BlockSpec((ge, C, D // 2), lambda b, hf, g: (g, b, hf)),
        out_shape=jax.ShapeDtypeStruct((E, B * C, D), BF16),
        scratch_shapes=[pltpu.VMEM((ht * stride, LANES), F32)],
        compiler_params=_params(("parallel", "parallel", "arbitrary"), VMEM_MIB_LARGE),
        name="gather_tokens",
    )(idx3, h_tm)


def _ffn_body(x_ref, wg_ref, wu_ref, wd_ref, o_ref, h_scr, *, tm, mc, tf, tn, half_tiles, n_f, n_n):
    fh = pl.program_id(1)
    j = pl.program_id(2)

    chunks = [slice(r, r + mc) for r in range(0, tm, mc)]

    @pl.when(jnp.logical_and(fh == 0, j == 0))
    def _():
        o_ref[...] = jnp.zeros(o_ref.shape, F32)

    @pl.when(j < n_f)
    def _():
        wg = wg_ref[0].astype(BF16)
        wu = wu_ref[0].astype(BF16)
        col0 = pl.multiple_of(j * tf, tf)
        for rs in chunks:
            x = x_ref[0, rs, :]
            a = jnp.dot(x, wg, preferred_element_type=F32)
            u = jnp.dot(x, wu, preferred_element_type=F32)
            h_scr[rs, pl.ds(col0, tf)] = (a * jax.nn.sigmoid(a) * u).astype(BF16)

    @pl.when(j >= n_f)
    def _():
        n = j - n_f
        wd = wd_ref[0].astype(BF16)
        tiles_per_step = tn // LANES
        steps_per_half = half_tiles // tiles_per_step
        half = n // steps_per_half
        row0 = (n % steps_per_half) * tiles_per_step
        for rs in chunks:
            y = jnp.dot(h_scr[rs, :], wd, preferred_element_type=F32)
            for c in range(tiles_per_step):
                rows = pl.ds(rs.start * half_tiles + row0 + c, mc, stride=half_tiles)
                o_ref[0, half, rows, :] = o_ref[0, half, rows, :] + y[:, c * LANES:(c + 1) * LANES]


def _ffn(cfg, xg, w_gate, w_up, w_down):
    E, D, F = cfg.n_experts, cfg.d_model, cfg.d_ff
    tm = cfg.batch * cfg.cap
    fhalf = F // 2
    tf = min(FFN_FF_TILE, fhalf)
    tn = FFN_OUT_TILE
    ht = D // (2 * LANES)
    n_f = fhalf // tf
    n_n = D // tn
    return pl.pallas_call(
        functools.partial(_ffn_body, tm=tm, mc=min(FFN_ROW_CHUNK, tm), tf=tf, tn=tn, half_tiles=ht,
                          n_f=n_f, n_n=n_n),
        grid=(E, 2, n_f + n_n),
        in_specs=[
            _resident((1, tm, D), lambda e, h, j: (e, 0, 0)),
            pl.BlockSpec((1, D, tf), lambda e, h, j: (e, 0, h * n_f + jnp.minimum(j, n_f - 1))),
            pl.BlockSpec((1, D, tf), lambda e, h, j: (e, 0, h * n_f + jnp.minimum(j, n_f - 1))),
            pl.BlockSpec((1, fhalf, tn), lambda e, h, j: (e, h, jnp.maximum(j - n_f, 0))),
        ],
        out_specs=pl.BlockSpec((1, 2, tm * ht, LANES), lambda e, h, j: (e, 0, 0, 0),
                               pipeline_mode=pl.Buffered(1)),
        out_shape=jax.ShapeDtypeStruct((E, 2, tm * ht, LANES), F32),
        scratch_shapes=[pltpu.VMEM((tm, fhalf), BF16)],
        compiler_params=_params(("parallel", "arbitrary", "arbitrary"), VMEM_MIB_FFN),
        name="expert_ffn",
    )(xg, w_gate, w_up, w_down)


def _combine_body(idx_ref, gate_ref, eo_ref, o_ref, *, cap, half_tiles, unroll):
    @pl.when(pl.program_id(2) == 0)
    def _():
        o_ref[...] = jnp.zeros(o_ref.shape, F32)

    def rows(g, carry):
        base = g * unroll
        dsts, vals = [], []
        for u in range(unroll):
            t = idx_ref[0, 0, base + u]
            dst = pl.multiple_of(t * half_tiles, half_tiles)
            src = pl.multiple_of((base + u) * half_tiles, half_tiles)
            dsts.append(dst)
            vals.append(o_ref[0, 0, pl.ds(dst, half_tiles), :]
                        + eo_ref[0, 0, pl.ds(src, half_tiles), :] * gate_ref[0, 0, t])
        for u in range(unroll):
            o_ref[0, 0, pl.ds(dsts[u], half_tiles), :] = vals[u]
        return carry

    lax.fori_loop(0, cap // unroll, rows, 0)


def _combine(cfg, idx3, gates3, eo_tm):
    B, S, E, C, D = cfg.batch, cfg.seq, cfg.n_experts, cfg.cap, cfg.d_model
    ht = D // (2 * LANES)
    return pl.pallas_call(
        functools.partial(_combine_body, cap=C, half_tiles=ht, unroll=COMBINE_ROWS_PER_GROUP),
        grid=(B, 2, E),
        in_specs=[
            pl.BlockSpec((1, 1, C), lambda b, h, e: (b * E + e, 0, 0), memory_space=pltpu.SMEM),
            pl.BlockSpec((1, 1, S), lambda b, h, e: (b * E + e, 0, 0), memory_space=pltpu.SMEM),
            pl.BlockSpec((1, 1, C * ht, LANES), lambda b, h, e: (e, h, b, 0)),
        ],
        out_specs=pl.BlockSpec((1, 1, S * ht, LANES), lambda b, h, e: (b, h, 0, 0)),
        out_shape=jax.ShapeDtypeStruct((B, 2, S * ht, LANES), F32),
        compiler_params=_params(("parallel", "parallel", "arbitrary"), VMEM_MIB_LARGE),
        name="combine",
    )(idx3, gates3, eo_tm)


def _final_body(f_ref, x_ref, g_ref, o_ref, *, tm, half_tiles, d_model):
    ss = jnp.zeros((tm, 1), F32)
    for j in range(2 * half_tiles):
        cols = slice(j * LANES, (j + 1) * LANES)
        y = x_ref[:, cols] + f_ref[0, j // half_tiles, pl.ds(j % half_tiles, tm, stride=half_tiles), :]
        o_ref[:, cols] = y
        ss = ss + jnp.sum(y * y, axis=-1, keepdims=True)
    o_ref[...] = o_ref[...] * lax.rsqrt(ss / d_model + EPS) * g_ref[...]


def _final(cfg, ffn_tm, x1, norm_final):
    T, D = x1.shape
    ht = D // (2 * LANES)
    tm = FINAL_ROWS
    sb = cfg.seq // tm
    return pl.pallas_call(
        functools.partial(_final_body, tm=tm, half_tiles=ht, d_model=D),
        grid=(T // tm,),
        in_specs=[
            pl.BlockSpec((1, 2, tm * ht, LANES), lambda i: (i // sb, 0, i % sb, 0)),
            pl.BlockSpec((tm, D), lambda i: (i, 0)),
            _resident((1, D), lambda i: (0, 0)),
        ],
        out_specs=pl.BlockSpec((tm, D), lambda i: (i, 0)),
        out_shape=jax.ShapeDtypeStruct((T, D), F32),
        compiler_params=_params(("parallel",), VMEM_MIB_DEFAULT),
        name="final_norm",
    )(ffn_tm, x1, norm_final)


def _rope_tables(seq):
    f32 = np.float32
    rows = seq // GRID_W
    row = np.repeat(np.arange(rows), GRID_W).astype(f32)
    col = np.tile(np.arange(GRID_W), rows).astype(f32)
    half = HEAD_DIM // 2
    inv_freq = np.power(f32(ROPE_THETA), -np.arange(0, half, 2, dtype=f32) / f32(half)).astype(f32)
    ang_r = row[:, None] * inv_freq[None, :]
    ang_c = col[:, None] * inv_freq[None, :]
    cos_t = np.concatenate([np.cos(ang_r), np.cos(ang_r), np.cos(ang_c), np.cos(ang_c)], axis=-1)
    sin_t = np.concatenate([-np.sin(ang_r), np.sin(ang_r), -np.sin(ang_c), np.sin(ang_c)], axis=-1)
    return jnp.asarray(cos_t, F32), jnp.asarray(sin_t, F32)


def _layer(cfg, x2, cos_t, sin_t, slopes, norm_mix, w_in, sink_a, q_norm_b, k_norm_b, w_out,
           norm_ffn, w_router, w_gate, w_up, w_down):
    B, S, E, C = cfg.batch, cfg.seq, cfg.n_experts, cfg.cap
    qa, ka, va, qb, kb, vb = _inproj(cfg, x2, norm_mix[None], w_in.astype(BF16),
                                     q_norm_b[None], k_norm_b[None], cos_t, sin_t)
    oa = _win_attention(cfg, sink_a, slopes, qa, ka, va)
    ob = _grid_attention(cfg, qb, kb, vb)
    w_out_bf = w_out.astype(BF16)
    n_a = H_A * HEAD_DIM
    wr_hi = w_router.astype(BF16)
    wr_lo = (w_router - wr_hi.astype(F32)).astype(BF16)
    x1, h_tm, aff = _outproj(cfg, oa, ob, x2, w_out_bf[:n_a], w_out_bf[n_a:], norm_ffn[None],
                             jnp.concatenate([wr_hi, wr_lo], axis=-1))
    aff_t = aff.reshape(B, S, E).transpose(0, 2, 1)
    idx = _topk(cfg, aff, aff_t)
    idx3 = idx.reshape(B * E, 1, C)
    gates3 = aff_t.reshape(B * E, 1, S)
    xg = _gather(cfg, idx3, h_tm)
    eo_tm = _ffn(cfg, xg, w_gate, w_up, w_down)
    ffn_tm = _combine(cfg, idx3, gates3, eo_tm)
    return x1, ffn_tm


def _forward(cfg, x, norm_mix, w_in, sink_a, q_norm_b, k_norm_b, w_out, norm_ffn,
             w_router, w_gate, w_up, w_down, norm_final):
    B, S, D = x.shape
    assert norm_mix.shape[0] == 1, "single-layer stack only"
    cos_t, sin_t = _rope_tables(S)
    slopes = jnp.asarray(2.0 ** (-8.0 * np.arange(1, H_A + 1) / H_A), dtype=F32)
    x2 = x.reshape(B * S, D)
    x1, ffn_tm = _layer(cfg, x2, cos_t, sin_t, slopes, norm_mix[0], w_in[0], sink_a[0],
                        q_norm_b[0], k_norm_b[0], w_out[0], norm_ffn[0], w_router[0],
                        w_gate[0], w_up[0], w_down[0])
    out = _final(cfg, ffn_tm, x1, norm_final[None])
    return out.reshape(B, S, D)


def kernel(x, norm_mix, w_in, sink_a, q_norm_b, k_norm_b, w_out, norm_ffn, w_router, w_gate, w_up,
           w_down, norm_final):
    B, S, D = x.shape
    E = w_router.shape[-1]
    cfg = Cfg(batch=B, seq=S, d_model=D, d_ff=w_gate.shape[-1], n_experts=E,
              cap=CAPACITY_FACTOR * S // E)
    return _forward(cfg, x, norm_mix, w_in, sink_a, q_norm_b, k_norm_b, w_out, norm_ffn,
                    w_router, w_gate, w_up, w_down, norm_final)
```

```python
import functools
from typing import NamedTuple

import numpy as np
import jax
import jax.numpy as jnp
from jax import lax
from jax.experimental import pallas as pl
from jax.experimental.pallas import tpu as pltpu

HEAD_DIM = 128
H_A, KV_A = 8, 2
H_B, KV_B = 8, 2
GROUP = 4
WINDOW = 128
BLOCK = 128
GRID_W = 64
ROPE_THETA = 10000.0
CAPACITY_FACTOR = 2
EPS = 1e-6
NEG_INF = -1e30
SCALE = HEAD_DIM ** -0.5
LOG2E = 1.4426950408889634
LANES = 128
SUBLANES = 8
N_PROJ_HEADS = H_A + 2 * KV_A + H_B + 2 * KV_B
MIB = 1024 * 1024

PROJ_ROWS = 512
PROJ_ROW_CHUNK = 256
FINAL_ROWS = 512
GRID_Q_ROWS = 256
GRID_TILES_PER_ITER = 4
WIN_BLOCKS_PER_ITER = 8
TOPK_PREFIX_ROWS = 256
TOPK_COUNT_ROWS = 512
GATHER_EXPERTS_PER_STEP = 4
COMBINE_ROWS_PER_GROUP = 8
FFN_ROW_CHUNK = 512
FFN_FF_TILE = 512
FFN_OUT_TILE = 256
VMEM_MIB_DEFAULT = 40
VMEM_MIB_LARGE = 48
VMEM_MIB_FFN = 62

F32 = jnp.float32
BF16 = jnp.bfloat16
I32 = jnp.int32


class Cfg(NamedTuple):
    batch: int
    seq: int
    d_model: int
    d_ff: int
    n_experts: int
    cap: int


def _params(sem, vmem_mib):
    return pltpu.CompilerParams(dimension_semantics=sem, vmem_limit_bytes=vmem_mib * MIB)


def _resident(shape, index_map):
    return pl.BlockSpec(shape, index_map, pipeline_mode=pl.Buffered(1))


def _inproj_body(x_ref, g_ref, w_ref, qn_ref, kn_ref, cos_ref, sin_ref,
                 qa_ref, ka_ref, va_ref, qb_ref, kb_ref, vb_ref):
    xf = x_ref[...]
    ms = jnp.mean(xf * xf, axis=-1, keepdims=True)
    h = (xf * lax.rsqrt(ms + EPS) * g_ref[...]).astype(BF16)
    n_a = (H_A + 2 * KV_A) * HEAD_DIM
    proj_b = jnp.dot(h, w_ref[:, n_a:], preferred_element_type=F32)
    proj_a = jnp.dot(h, w_ref[:, :n_a], preferred_element_type=F32)
    cos = cos_ref[...]
    sin = sin_ref[...]
    lane = lax.broadcasted_iota(I32, (1, LANES), 1)
    first = (lane % 64) < 32

    def head(c):
        if c * HEAD_DIM < n_a:
            return proj_a[:, c * HEAD_DIM:(c + 1) * HEAD_DIM]
        return proj_b[:, c * HEAD_DIM - n_a:(c + 1) * HEAD_DIM - n_a]

    def norm_rope(t, gain):
        m = jnp.mean(t * t, axis=-1, keepdims=True)
        tn = t * lax.rsqrt(m + EPS) * gain
        partner = jnp.where(first, pltpu.roll(tn, 96, 1), pltpu.roll(tn, 32, 1))
        return tn * cos + partner * sin

    c = H_A + 2 * KV_A
    for hh in range(H_B):
        qb_ref[hh] = (norm_rope(head(c + hh), qn_ref[...]) * (SCALE * LOG2E)).astype(BF16)
    c += H_B
    for hh in range(KV_B):
        kb_ref[hh] = norm_rope(head(c + hh), kn_ref[...]).astype(BF16)
    c += KV_B
    for hh in range(KV_B):
        vb_ref[hh] = head(c + hh).astype(BF16)
    c = 0
    for hh in range(H_A):
        qa_ref[hh] = (head(c + hh) * (SCALE * LOG2E)).astype(BF16)
    c += H_A
    for hh in range(KV_A):
        ka_ref[hh] = head(c + hh).astype(BF16)
    c += KV_A
    for hh in range(KV_A):
        va_ref[hh] = head(c + hh).astype(BF16)


def _inproj(cfg, x2, norm_mix, w_in_bf, q_norm, k_norm, cos_t, sin_t):
    T, D = x2.shape
    tm = PROJ_ROWS
    sb = cfg.seq // tm
    heads = lambda n: jax.ShapeDtypeStruct((n, T, HEAD_DIM), BF16)
    hspec = lambda n: pl.BlockSpec((n, tm, HEAD_DIM), lambda i: (0, i, 0))
    return pl.pallas_call(
        _inproj_body,
        grid=(T // tm,),
        in_specs=[
            pl.BlockSpec((tm, D), lambda i: (i, 0)),
            _resident((1, D), lambda i: (0, 0)),
            _resident((D, N_PROJ_HEADS * HEAD_DIM), lambda i: (0, 0)),
            _resident((1, HEAD_DIM), lambda i: (0, 0)),
            _resident((1, HEAD_DIM), lambda i: (0, 0)),
            pl.BlockSpec((tm, HEAD_DIM), lambda i: (i % sb, 0)),
            pl.BlockSpec((tm, HEAD_DIM), lambda i: (i % sb, 0)),
        ],
        out_specs=[hspec(H_A), hspec(KV_A), hspec(KV_A), hspec(H_B), hspec(KV_B), hspec(KV_B)],
        out_shape=[heads(H_A), heads(KV_A), heads(KV_A), heads(H_B), heads(KV_B), heads(KV_B)],
        compiler_params=_params(("parallel",), VMEM_MIB_LARGE),
        name="inproj",
    )(x2, norm_mix, w_in_bf, q_norm, k_norm, cos_t, sin_t)


def _win_body(sink_ref, slope_ref, q_ref, k_ref, v_ref, o_ref, kt_scr, vx_scr, bias_scr, sink_scr,
              *, seq, unroll):
    kvh = pl.program_id(1)
    span = 3 * BLOCK
    rows = GROUP * BLOCK
    nb = seq // BLOCK

    kt_scr[...] = k_ref[0].astype(F32).T.astype(BF16)
    vx_scr[:, :HEAD_DIM] = v_ref[0]
    vx_scr[:, HEAD_DIM:] = jnp.ones((seq, HEAD_DIM), BF16)
    qi = lax.broadcasted_iota(I32, (BLOCK, span), 0)
    kj = lax.broadcasted_iota(I32, (BLOCK, span), 1)
    for g in range(GROUP):
        hq = kvh * GROUP + g
        sink_scr[g * BLOCK:(g + 1) * BLOCK, :] = jnp.full((BLOCK, LANES), sink_ref[hq], F32) * LOG2E
        for place in range(3):
            dist = jnp.abs(qi + place * BLOCK - kj)
            bias = jnp.where(dist <= WINDOW, (slope_ref[hq] * dist.astype(F32)) * (-LOG2E), NEG_INF)
            bias_scr[place, g * BLOCK:(g + 1) * BLOCK, :] = bias

    def block(n):
        ws = pl.multiple_of(jnp.clip((n - 1) * BLOCK, 0, seq - span), BLOCK)
        q0 = pl.multiple_of(n * BLOCK, BLOCK)
        place = jnp.where(n == 0, 0, jnp.where(n == nb - 1, 2, 1))
        q = q_ref[:, pl.ds(q0, BLOCK), :].reshape(rows, HEAD_DIM)
        s = jnp.dot(q, kt_scr[:, pl.ds(ws, span)], preferred_element_type=F32) + bias_scr[place]
        sink = sink_scr[:, :1]
        m = jnp.maximum(jnp.max(s, axis=-1, keepdims=True), sink)
        p = jnp.exp2(s - m).astype(BF16)
        ox = jnp.dot(p, vx_scr[pl.ds(ws, span), :], preferred_element_type=F32)
        o = ox[:, :HEAD_DIM] / (ox[:, HEAD_DIM:] + jnp.exp2(sink - m))
        for g in range(GROUP):
            o_ref[pl.ds(q0, BLOCK), g * HEAD_DIM:(g + 1) * HEAD_DIM] = (
                o[g * BLOCK:(g + 1) * BLOCK].astype(BF16))

    def step(i, carry):
        for u in range(unroll):
            block(i * unroll + u)
        return carry

    lax.fori_loop(0, nb // unroll, step, 0)


def _win_attention(cfg, sink, slopes, qa, ka, va):
    S = cfg.seq
    T = cfg.batch * S
    assert S // BLOCK >= 4
    smem = pl.BlockSpec(memory_space=pltpu.SMEM)
    rows = GROUP * BLOCK
    return pl.pallas_call(
        functools.partial(_win_body, seq=S, unroll=WIN_BLOCKS_PER_ITER),
        grid=(cfg.batch, KV_A),
        in_specs=[
            smem, smem,
            pl.BlockSpec((GROUP, S, HEAD_DIM), lambda b, k: (k, b, 0)),
            pl.BlockSpec((1, S, HEAD_DIM), lambda b, k: (k, b, 0)),
            pl.BlockSpec((1, S, HEAD_DIM), lambda b, k: (k, b, 0)),
        ],
        out_specs=pl.BlockSpec((S, GROUP * HEAD_DIM), lambda b, k: (b, k)),
        out_shape=jax.ShapeDtypeStruct((T, H_A * HEAD_DIM), BF16),
        scratch_shapes=[pltpu.VMEM((HEAD_DIM, S), BF16), pltpu.VMEM((S, 2 * HEAD_DIM), BF16),
                        pltpu.VMEM((3, rows, 3 * BLOCK), F32), pltpu.VMEM((rows, LANES), F32)],
        compiler_params=_params(("parallel", "parallel"), VMEM_MIB_DEFAULT),
        name="win_attention",
    )(sink, slopes, qa, ka, va)


def _grid_body(q_ref, k_ref, v_ref, o_ref, kt_scr, vx_scr, *, seq, tq, unroll):
    kt_scr[...] = k_ref[0].astype(F32).T.astype(BF16)
    vx_scr[:, :HEAD_DIM] = v_ref[0]
    vx_scr[:, HEAD_DIM:] = jnp.ones((seq, HEAD_DIM), BF16)

    def tile(g, r0):
        q = q_ref[g, pl.ds(r0, tq), :]
        s = jnp.dot(q, kt_scr[...], preferred_element_type=F32)
        m = jnp.max(s, axis=-1, keepdims=True)
        p = jnp.exp2(s - m).astype(BF16)
        ox = jnp.dot(p, vx_scr[...], preferred_element_type=F32)
        o = ox[:, :HEAD_DIM] / ox[:, HEAD_DIM:]
        o_ref[pl.ds(r0, tq), g * HEAD_DIM:(g + 1) * HEAD_DIM] = o.astype(BF16)

    def step(i, carry):
        for u in range(unroll):
            r0 = pl.multiple_of((i * unroll + u) * tq, tq)
            for g in range(GROUP):
                tile(g, r0)
        return carry

    lax.fori_loop(0, seq // (tq * unroll), step, 0)


def _grid_attention(cfg, qb, kb, vb):
    S = cfg.seq
    T = cfg.batch * S
    tq = GRID_Q_ROWS
    return pl.pallas_call(
        functools.partial(_grid_body, seq=S, tq=tq, unroll=GRID_TILES_PER_ITER),
        grid=(cfg.batch, KV_B),
        in_specs=[
            pl.BlockSpec((GROUP, S, HEAD_DIM), lambda b, k: (k, b, 0)),
            pl.BlockSpec((1, S, HEAD_DIM), lambda b, k: (k, b, 0)),
            pl.BlockSpec((1, S, HEAD_DIM), lambda b, k: (k, b, 0)),
        ],
        out_specs=pl.BlockSpec((S, GROUP * HEAD_DIM), lambda b, k: (b, k)),
        out_shape=jax.ShapeDtypeStruct((T, H_B * HEAD_DIM), BF16),
        scratch_shapes=[pltpu.VMEM((HEAD_DIM, S), BF16), pltpu.VMEM((S, 2 * HEAD_DIM), BF16)],
        compiler_params=_params(("parallel", "parallel"), VMEM_MIB_LARGE),
        name="grid_attention",
    )(qb, kb, vb)


def _outproj_body(oa_ref, ob_ref, x_ref, wa_ref, wb_ref, g_ref, wr_ref,
                  x1_ref, h_ref, aff_ref, *, tm, mc, half_tiles, n_exp):
    for c in range(tm // mc):
        rows = slice(c * mc, (c + 1) * mc)
        mix = jnp.dot(oa_ref[rows, :], wa_ref[...], preferred_element_type=F32)
        mix = mix + jnp.dot(ob_ref[rows, :], wb_ref[...], preferred_element_type=F32)
        x1 = x_ref[rows, :] + mix
        x1_ref[rows, :] = x1
        ms = jnp.mean(x1 * x1, axis=-1, keepdims=True)
        h = x1 * lax.rsqrt(ms + EPS) * g_ref[...]
        h_hi = h.astype(BF16)
        h_lo = (h - h_hi.astype(F32)).astype(BF16)
        r = (jnp.dot(h_hi, wr_ref[...], preferred_element_type=F32)
             + jnp.dot(h_lo, wr_ref[...], preferred_element_type=F32))
        logits = r[:, :n_exp] + r[:, n_exp:]
        e = jnp.exp(logits - jnp.max(logits, axis=-1, keepdims=True))
        aff_ref[rows, :] = e / jnp.sum(e, axis=-1, keepdims=True)
        for half in range(2):
            for j in range(half_tiles):
                c0 = (half * half_tiles + j) * LANES
                h_ref[0, half, pl.ds(c * mc * half_tiles + j, mc, stride=half_tiles), :] = (
                    h[:, c0:c0 + LANES])


def _outproj(cfg, oa, ob, x2, w_a, w_b, norm_ffn, w_router_parts):
    T, D = x2.shape
    E = cfg.n_experts
    S = cfg.seq
    tm = PROJ_ROWS
    sb = S // tm
    ht = D // (2 * LANES)
    return pl.pallas_call(
        functools.partial(_outproj_body, tm=tm, mc=PROJ_ROW_CHUNK, half_tiles=ht, n_exp=E),
        grid=(T // tm,),
        in_specs=[
            pl.BlockSpec((tm, oa.shape[1]), lambda i: (i, 0)),
            pl.BlockSpec((tm, ob.shape[1]), lambda i: (i, 0)),
            pl.BlockSpec((tm, D), lambda i: (i, 0)),
            _resident(w_a.shape, lambda i: (0, 0)),
            _resident(w_b.shape, lambda i: (0, 0)),
            _resident((1, D), lambda i: (0, 0)),
            _resident((D, 2 * E), lambda i: (0, 0)),
        ],
        out_specs=[
            pl.BlockSpec((tm, D), lambda i: (i, 0)),
            pl.BlockSpec((1, 2, tm * ht, LANES), lambda i: (i // sb, 0, i % sb, 0)),
            pl.BlockSpec((tm, E), lambda i: (i, 0)),
        ],
        out_shape=[
            jax.ShapeDtypeStruct((T, D), F32),
            jax.ShapeDtypeStruct((cfg.batch, 2, S * ht, LANES), F32),
            jax.ShapeDtypeStruct((T, E), F32),
        ],
        compiler_params=_params(("parallel",), VMEM_MIB_DEFAULT),
        name="outproj_router",
    )(oa, ob, x2, w_a, w_b, norm_ffn, w_router_parts)


def _topk_body(aff_ref, afft_ref, idx_ref, c_scr, *, seq, cap, n_exp):
    a = aff_ref[...]
    bits = lax.bitcast_convert_type(a, I32)
    bits_t = lax.bitcast_convert_type(afft_ref[0], I32)

    def bisect(i, lo):
        cand = lo | jnp.left_shift(jnp.int32(1), 30 - i)
        cnt = jnp.sum((bits_t >= cand).astype(I32), axis=1, keepdims=True)
        return jnp.where(cnt >= cap, cand, lo)

    thr_col = lax.fori_loop(0, 31, bisect, jnp.zeros((n_exp, 1), I32))
    diag = (lax.broadcasted_iota(I32, (n_exp, n_exp), 0)
            == lax.broadcasted_iota(I32, (n_exp, n_exp), 1))
    thr = jnp.sum(jnp.where(diag, thr_col, 0), axis=0, keepdims=True)
    gt = bits > thr
    eq = bits == thr
    need = (cap - jnp.sum(gt.astype(I32), axis=0, keepdims=True)).astype(F32)

    rc = TOPK_PREFIX_ROWS
    r_i = lax.broadcasted_iota(I32, (rc, rc), 0)
    c_i = lax.broadcasted_iota(I32, (rc, rc), 1)
    lower = (c_i <= r_i).astype(BF16)

    def prefix(mask_f32):
        carry = jnp.zeros((1, n_exp), F32)
        for k in range(seq // rc):
            blk = mask_f32[k * rc:(k + 1) * rc, :]
            inc = jnp.dot(lower, blk.astype(BF16), preferred_element_type=F32) + carry
            c_scr[k * rc:(k + 1) * rc, :] = inc
            carry = inc[rc - 1:rc, :]
        return c_scr[...]

    eq_f = eq.astype(F32)
    tie_rank = prefix(eq_f) - eq_f
    sel = jnp.logical_or(gt, jnp.logical_and(eq, tie_rank < need))
    prefix(sel.astype(F32))

    r_row = lax.broadcasted_iota(I32, (1, cap), 1).astype(F32)
    rows = TOPK_COUNT_ROWS
    for e in range(n_exp):
        def count(k, acc, e=e):
            r0 = pl.multiple_of(k * rows, rows)
            col = c_scr[pl.ds(r0, rows), e:e + 1]
            hit = (col <= r_row).astype(F32)
            return acc + jnp.sum(hit.reshape(rows // SUBLANES, SUBLANES, cap), axis=0)

        acc = lax.fori_loop(0, seq // rows, count, jnp.zeros((SUBLANES, cap), F32))
        idx_ref[0, e:e + 1, :] = jnp.sum(acc, axis=0, keepdims=True).astype(I32)


def _topk(cfg, aff, aff_t):
    S, E, C = cfg.seq, cfg.n_experts, cfg.cap
    return pl.pallas_call(
        functools.partial(_topk_body, seq=S, cap=C, n_exp=E),
        grid=(cfg.batch,),
        in_specs=[pl.BlockSpec((S, E), lambda b: (b, 0)),
                  pl.BlockSpec((1, E, S), lambda b: (b, 0, 0))],
        out_specs=pl.BlockSpec((1, E, C), lambda b: (b, 0, 0)),
        out_shape=jax.ShapeDtypeStruct((cfg.batch, E, C), I32),
        scratch_shapes=[pltpu.VMEM((S, E), F32)],
        compiler_params=_params(("parallel",), VMEM_MIB_DEFAULT),
        name="topk",
    )(aff, aff_t)


def _gather_body(idx_ref, h_ref, o_ref, tile_scr, *, cap, half_tiles, stride, group):
    def expert(x, carry):
        for m in range(cap):
            t = idx_ref[x, 0, m]
            row = pl.multiple_of(t * half_tiles, half_tiles)
            tile_scr[pl.ds(m, half_tiles, stride=stride), :] = h_ref[0, 0, pl.ds(row, half_tiles), :]
        for j in range(half_tiles):
            o_ref[x, :, j * LANES:(j + 1) * LANES] = tile_scr[pl.ds(j * stride, cap), :].astype(BF16)
        return carry

    lax.fori_loop(0, group, expert, 0)


def _gather(cfg, idx3, h_tm):
    B, S, E, C, D = cfg.batch, cfg.seq, cfg.n_experts, cfg.cap, cfg.d_model
    ht = D // (2 * LANES)
    stride = C + SUBLANES
    ge = GATHER_EXPERTS_PER_STEP
    return pl.pallas_call(
        functools.partial(_gather_body, cap=C, half_tiles=ht, stride=stride, group=ge),
        grid=(B, 2, E // ge),
        in_specs=[
            pl.BlockSpec((ge, 1, C), lambda b, hf, g: (b * (E // ge) + g, 0, 0), memory_space=pltpu.SMEM),
            pl.BlockSpec((1, 1, S * ht, LANES), lambda b, hf, g: (b, hf, 0, 0)),
        ],
        out_specs=pl.BlockSpec((ge, C, D // 2), lambda b, hf, g: (g, b, hf)),
        out_shape=jax.ShapeDtypeStruct((E, B * C, D), BF16),
        scratch_shapes=[pltpu.VMEM((ht * stride, LANES), F32)],
        compiler_params=_params(("parallel", "parallel", "arbitrary"), VMEM_MIB_LARGE),
        name="gather_tokens",
    )(idx3, h_tm)


def _ffn_body(x_ref, wg_ref, wu_ref, wd_ref, o_ref, h_scr, *, tm, mc, tf, tn, half_tiles, n_f, n_n):
    fh = pl.program_id(1)
    j = pl.program_id(2)

    chunks = [slice(r, r + mc) for r in range(0, tm, mc)]

    @pl.when(j < n_f)
    def _():
        wg = wg_ref[0].astype(BF16)
        wu = wu_ref[0].astype(BF16)
        col0 = pl.multiple_of(j * tf, tf)
        for rs in chunks:
            x = x_ref[0, rs, :]
            a = jnp.dot(x, wg, preferred_element_type=F32)
            u = jnp.dot(x, wu, preferred_element_type=F32)
            h_scr[rs, pl.ds(col0, tf)] = (a * jax.nn.sigmoid(a) * u).astype(BF16)

    def down_proj(accumulate):
        n = j - n_f
        wd = wd_ref[0].astype(BF16)
        tiles_per_step = tn // LANES
        steps_per_half = half_tiles // tiles_per_step
        half = n // steps_per_half
        row0 = (n % steps_per_half) * tiles_per_step
        for rs in chunks:
            y = jnp.dot(h_scr[rs, :], wd, preferred_element_type=F32)
            for c in range(tiles_per_step):
                rows = pl.ds(rs.start * half_tiles + row0 + c, mc, stride=half_tiles)
                yc = y[:, c * LANES:(c + 1) * LANES]
                o_ref[0, half, rows, :] = o_ref[0, half, rows, :] + yc if accumulate else yc

    @pl.when(jnp.logical_and(j >= n_f, fh == 0))
    def _():
        down_proj(accumulate=False)

    @pl.when(jnp.logical_and(j >= n_f, fh > 0))
    def _():
        down_proj(accumulate=True)


def _ffn(cfg, xg, w_gate, w_up, w_down):
    E, D, F = cfg.n_experts, cfg.d_model, cfg.d_ff
    tm = cfg.batch * cfg.cap
    fhalf = F // 2
    tf = min(FFN_FF_TILE, fhalf)
    tn = FFN_OUT_TILE
    ht = D // (2 * LANES)
    n_f = fhalf // tf
    n_n = D // tn
    return pl.pallas_call(
        functools.partial(_ffn_body, tm=tm, mc=min(FFN_ROW_CHUNK, tm), tf=tf, tn=tn, half_tiles=ht,
                          n_f=n_f, n_n=n_n),
        grid=(E, 2, n_f + n_n),
        in_specs=[
            _resident((1, tm, D), lambda e, h, j: (e, 0, 0)),
            pl.BlockSpec((1, D, tf), lambda e, h, j: (e, 0, h * n_f + jnp.minimum(j, n_f - 1))),
            pl.BlockSpec((1, D, tf), lambda e, h, j: (e, 0, h * n_f + jnp.minimum(j, n_f - 1))),
            pl.BlockSpec((1, fhalf, tn), lambda e, h, j: (e, h, jnp.maximum(j - n_f, 0))),
        ],
        out_specs=pl.BlockSpec((1, 2, tm * ht, LANES), lambda e, h, j: (e, 0, 0, 0),
                               pipeline_mode=pl.Buffered(1)),
        out_shape=jax.ShapeDtypeStruct((E, 2, tm * ht, LANES), F32),
        scratch_shapes=[pltpu.VMEM((tm, fhalf), BF16)],
        compiler_params=_params(("parallel", "arbitrary", "arbitrary"), VMEM_MIB_FFN),
        name="expert_ffn",
    )(xg, w_gate, w_up, w_down)


def _combine_body(idx_ref, gate_ref, eo_ref, o_ref, *, cap, half_tiles, unroll):
    @pl.when(pl.program_id(2) == 0)
    def _():
        o_ref[...] = jnp.zeros(o_ref.shape, F32)

    def rows(g, carry):
        base = g * unroll
        dsts, vals = [], []
        for u in range(unroll):
            t = idx_ref[0, 0, base + u]
            dst = pl.multiple_of(t * half_tiles, half_tiles)
            src = pl.multiple_of((base + u) * half_tiles, half_tiles)
            dsts.append(dst)
            vals.append(o_ref[0, 0, pl.ds(dst, half_tiles), :]
                        + eo_ref[0, 0, pl.ds(src, half_tiles), :] * gate_ref[0, 0, t])
        for u in range(unroll):
            o_ref[0, 0, pl.ds(dsts[u], half_tiles), :] = vals[u]
        return carry

    lax.fori_loop(0, cap // unroll, rows, 0)


def _combine(cfg, idx3, gates3, eo_tm):
    B, S, E, C, D = cfg.batch, cfg.seq, cfg.n_experts, cfg.cap, cfg.d_model
    ht = D // (2 * LANES)
    return pl.pallas_call(
        functools.partial(_combine_body, cap=C, half_tiles=ht, unroll=COMBINE_ROWS_PER_GROUP),
        grid=(B, 2, E),
        in_specs=[
            pl.BlockSpec((1, 1, C), lambda b, h, e: (b * E + e, 0, 0), memory_space=pltpu.SMEM),
            pl.BlockSpec((1, 1, S), lambda b, h, e: (b * E + e, 0, 0), memory_space=pltpu.SMEM),
            pl.BlockSpec((1, 1, C * ht, LANES), lambda b, h, e: (e, h, b, 0)),
        ],
        out_specs=pl.BlockSpec((1, 1, S * ht, LANES), lambda b, h, e: (b, h, 0, 0)),
        out_shape=jax.ShapeDtypeStruct((B, 2, S * ht, LANES), F32),
        compiler_params=_params(("parallel", "parallel", "arbitrary"), VMEM_MIB_LARGE),
        name="combine",
    )(idx3, gates3, eo_tm)


def _final_body(f_ref, x_ref, g_ref, o_ref, *, tm, half_tiles, d_model):
    ss = jnp.zeros((tm, 1), F32)
    for j in range(2 * half_tiles):
        cols = slice(j * LANES, (j + 1) * LANES)
        y = x_ref[:, cols] + f_ref[0, j // half_tiles, pl.ds(j % half_tiles, tm, stride=half_tiles), :]
        o_ref[:, cols] = y
        ss = ss + jnp.sum(y * y, axis=-1, keepdims=True)
    o_ref[...] = o_ref[...] * lax.rsqrt(ss / d_model + EPS) * g_ref[...]


def _final(cfg, ffn_tm, x1, norm_final):
    T, D = x1.shape
    ht = D // (2 * LANES)
    tm = FINAL_ROWS
    sb = cfg.seq // tm
    return pl.pallas_call(
        functools.partial(_final_body, tm=tm, half_tiles=ht, d_model=D),
        grid=(T // tm,),
        in_specs=[
            pl.BlockSpec((1, 2, tm * ht, LANES), lambda i: (i // sb, 0, i % sb, 0)),
            pl.BlockSpec((tm, D), lambda i: (i, 0)),
            _resident((1, D), lambda i: (0, 0)),
        ],
        out_specs=pl.BlockSpec((tm, D), lambda i: (i, 0)),
        out_shape=jax.ShapeDtypeStruct((T, D), F32),
        compiler_params=_params(("parallel",), VMEM_MIB_DEFAULT),
        name="final_norm",
    )(ffn_tm, x1, norm_final)


def _rope_tables(seq):
    f32 = np.float32
    rows = seq // GRID_W
    row = np.repeat(np.arange(rows), GRID_W).astype(f32)
    col = np.tile(np.arange(GRID_W), rows).astype(f32)
    half = HEAD_DIM // 2
    inv_freq = np.power(f32(ROPE_THETA), -np.arange(0, half, 2, dtype=f32) / f32(half)).astype(f32)
    ang_r = row[:, None] * inv_freq[None, :]
    ang_c = col[:, None] * inv_freq[None, :]
    cos_t = np.concatenate([np.cos(ang_r), np.cos(ang_r), np.cos(ang_c), np.cos(ang_c)], axis=-1)
    sin_t = np.concatenate([-np.sin(ang_r), np.sin(ang_r), -np.sin(ang_c), np.sin(ang_c)], axis=-1)
    return jnp.asarray(cos_t, F32), jnp.asarray(sin_t, F32)


def _layer(cfg, x2, cos_t, sin_t, slopes, norm_mix, w_in, sink_a, q_norm_b, k_norm_b, w_out,
           norm_ffn, w_router, w_gate, w_up, w_down):
    B, S, E, C = cfg.batch, cfg.seq, cfg.n_experts, cfg.cap
    qa, ka, va, qb, kb, vb = _inproj(cfg, x2, norm_mix[None], w_in.astype(BF16),
                                     q_norm_b[None], k_norm_b[None], cos_t, sin_t)
    oa = _win_attention(cfg, sink_a, slopes, qa, ka, va)
    ob = _grid_attention(cfg, qb, kb, vb)
    w_out_bf = w_out.astype(BF16)
    n_a = H_A * HEAD_DIM
    wr_hi = w_router.astype(BF16)
    wr_lo = (w_router - wr_hi.astype(F32)).astype(BF16)
    x1, h_tm, aff = _outproj(cfg, oa, ob, x2, w_out_bf[:n_a], w_out_bf[n_a:], norm_ffn[None],
                             jnp.concatenate([wr_hi, wr_lo], axis=-1))
    aff_t = aff.reshape(B, S, E).transpose(0, 2, 1)
    idx = _topk(cfg, aff, aff_t)
    idx3 = idx.reshape(B * E, 1, C)
    gates3 = aff_t.reshape(B * E, 1, S)
    xg = _gather(cfg, idx3, h_tm)
    eo_tm = _ffn(cfg, xg, w_gate, w_up, w_down)
    ffn_tm = _combine(cfg, idx3, gates3, eo_tm)
    return x1, ffn_tm


def _forward(cfg, x, norm_mix, w_in, sink_a, q_norm_b, k_norm_b, w_out, norm_ffn,
             w_router, w_gate, w_up, w_down, norm_final):
    B, S, D = x.shape
    assert norm_mix.shape[0] == 1, "single-layer stack only"
    cos_t, sin_t = _rope_tables(S)
    slopes = jnp.asarray(2.0 ** (-8.0 * np.arange(1, H_A + 1) / H_A), dtype=F32)
    x2 = x.reshape(B * S, D)
    x1, ffn_tm = _layer(cfg, x2, cos_t, sin_t, slopes, norm_mix[0], w_in[0], sink_a[0],
                        q_norm_b[0], k_norm_b[0], w_out[0], norm_ffn[0], w_router[0],
                        w_gate[0], w_up[0], w_down[0])
    out = _final(cfg, ffn_tm, x1, norm_final[None])
    return out.reshape(B, S, D)


def kernel(x, norm_mix, w_in, sink_a, q_norm_b, k_norm_b, w_out, norm_ffn, w_router, w_gate, w_up,
           w_down, norm_final):
    B, S, D = x.shape
    E = w_router.shape[-1]
    cfg = Cfg(batch=B, seq=S, d_model=D, d_ff=w_gate.shape[-1], n_experts=E,
              cap=CAPACITY_FACTOR * S // E)
    return _forward(cfg, x, norm_mix, w_in, sink_a, q_norm_b, k_norm_b, w_out, norm_ffn,
                    w_router, w_gate, w_up, w_down, norm_final)
```
